```python
import math
import jax
import jax.numpy as jnp
from jax import lax
import numpy as np

D_MODEL = 2048
BATCH = 4
SEQ = 4096
DEPTH = 4

N_META = 16
N_HEADS = 8
HEAD_DIM = 128
V_DIM = 2 * HEAD_DIM
ROT_DIM = HEAD_DIM // 4
ROPE_THETA = 500000.0
Q_BLOCK = 128
ATTN_QK = N_HEADS * 2 * HEAD_DIM
ATTN_V = N_HEADS * V_DIM
LRU_WIDTH = 5 * D_MODEL // 4
LRU_BLOCKS = 16
LRU_BLOCK_W = LRU_WIDTH // LRU_BLOCKS
CONV_W = 4
LRU_C = 8.0
D_FF = ((8 * D_MODEL // 3 + 255) // 256) * 256
N_IN = 2 * ATTN_QK + ATTN_V + 2 * LRU_WIDTH + 2 * D_MODEL
EPS = 1e-6

kernel_name = 'hybrid_diffattn_rglru_macaron_meta'


def rms_norm(x, g):
    xf = x.astype(jnp.float32)
    y = xf * lax.rsqrt(jnp.mean(xf * xf, axis=-1, keepdims=True) + EPS)
    return (y * g.astype(jnp.float32)).astype(x.dtype)


def swiglu(u, w_gate, w_up, w_down):
    return (jax.nn.silu(u @ w_gate) * (u @ w_up)) @ w_down


def rotary_tables(t):
    inv_freq = ROPE_THETA ** (-jnp.arange(0, ROT_DIM, 2, dtype=jnp.float32) / ROT_DIM)
    ang = jnp.arange(t, dtype=jnp.float32)[:, None] * inv_freq[None, :]
    return jnp.cos(ang), jnp.sin(ang)


def apply_partial_rope(x, cos, sin):
    half = ROT_DIM // 2
    xr = x[..., :ROT_DIM].astype(jnp.float32)
    x1, x2 = xr[..., :half], xr[..., half:]
    c = cos[None, :, None, None, :]
    s = sin[None, :, None, None, :]
    rot = jnp.concatenate([x1 * c - x2 * s, x2 * c + x1 * s], axis=-1).astype(x.dtype)
    return jnp.concatenate([rot, x[..., ROT_DIM:]], axis=-1)


def diff_attention(q, k, v, lam, lam_init, subln, cos, sin):
    b, t = q.shape[0], q.shape[1]
    q = apply_partial_rope(q.reshape(b, t, N_HEADS, 2, HEAD_DIM), cos, sin)
    k = apply_partial_rope(k.reshape(b, t, N_HEADS, 2, HEAD_DIM), cos, sin)
    qf = q.astype(jnp.float32) * (HEAD_DIM ** -0.5)
    kf = k.astype(jnp.float32)
    vf = v.reshape(b, t, N_HEADS, V_DIM).astype(jnp.float32)
    starts = [0] + list(range(N_META, t, Q_BLOCK))
    ends = starts[1:] + [t]
    outs = []
    for s, e in zip(starts, ends):
        sc = jnp.einsum('bqhcd,bkhcd->bhcqk', qf[:, s:e], kf[:, :e])
        causal = jnp.arange(e)[None, :] <= jnp.arange(s, e)[:, None]
        p = jax.nn.softmax(jnp.where(causal, sc, -jnp.inf), axis=-1)
        p = p[:, :, 0] - lam * p[:, :, 1]
        outs.append(jnp.einsum('bhqk,bkhe->bqhe', p, vf[:, :e]))
    o = jnp.concatenate(outs, axis=1)
    o = o * lax.rsqrt(jnp.mean(o * o, axis=-1, keepdims=True) + EPS)
    o = o * subln.astype(jnp.float32) * (1.0 - lam_init)
    return o.reshape(b, t, ATTN_V).astype(v.dtype)


def _lru_combine(e1, e2):
    a1, b1 = e1
    a2, b2 = e2
    return a1 * a2, a2 * b1 + b2


def rglru_branch(xr, yr, conv_w, conv_b, gx_w, gx_b, ga_w, ga_b, a_param):
    b, t, _ = xr.shape
    xp = jnp.pad(xr, ((0, 0), (CONV_W - 1, 0), (0, 0)))
    xc = conv_b + sum(xp[:, CONV_W - 1 - j: CONV_W - 1 - j + t] * conv_w[j] for j in range(CONV_W))
    xb = xc.reshape(b, t, LRU_BLOCKS, LRU_BLOCK_W)
    r = jax.nn.sigmoid(jnp.einsum('btnk,nkj->btnj', xb, ga_w).reshape(b, t, LRU_WIDTH) + ga_b)
    i = jax.nn.sigmoid(jnp.einsum('btnk,nkj->btnj', xb, gx_w).reshape(b, t, LRU_WIDTH) + gx_b)
    log_a = LRU_C * r.astype(jnp.float32) * jax.nn.log_sigmoid(a_param.astype(jnp.float32))
    a = jnp.exp(log_a)
    mult = jnp.sqrt(-jnp.expm1(2.0 * log_a))
    bx = mult * (i * xc).astype(jnp.float32)
    _, h = lax.associative_scan(_lru_combine, (a, bx), axis=1)
    return h.astype(xr.dtype) * jax.nn.gelu(yr)


def hybrid_mixer(u, w_in, b_gate, lq1, lk1, lq2, lk2, subln, conv_w, conv_b,
                 gx_w, gx_b, ga_w, ga_b, a_param, w_ba, w_bl, w_o, cos, sin, lam_init):
    z = u @ w_in
    cuts = [ATTN_QK, 2 * ATTN_QK, 2 * ATTN_QK + ATTN_V,
            2 * ATTN_QK + ATTN_V + LRU_WIDTH, 2 * ATTN_QK + ATTN_V + 2 * LRU_WIDTH]
    q, k, v, xr, yr, zg = jnp.split(z, cuts, axis=-1)
    g = jax.nn.sigmoid(zg + b_gate)
    g_attn, g_lru = g[..., :D_MODEL], g[..., D_MODEL:]
    lam = (jnp.exp(jnp.sum(lq1.astype(jnp.float32) * lk1.astype(jnp.float32)))
           - jnp.exp(jnp.sum(lq2.astype(jnp.float32) * lk2.astype(jnp.float32))) + lam_init)
    o_attn = diff_attention(q, k, v, lam, lam_init, subln, cos, sin) @ w_ba
    o_lru = rglru_branch(xr, yr, conv_w, conv_b, gx_w, gx_b, ga_w, ga_b, a_param) @ w_bl
    return (g_attn * o_attn + g_lru * o_lru) @ w_o


def setup_inputs(seed: int = 0) -> dict:
    key = jax.random.key(seed)
    ks = iter(jax.random.split(key, 40))

    def w(shape, fan_in):
        return jax.random.normal(next(ks), shape, jnp.float32) * (fan_in ** -0.5)

    def gain(shape):
        return 1.0 + 0.02 * jax.random.normal(next(ks), shape, jnp.float32)

    def small(shape, scale=0.01):
        return scale * jax.random.normal(next(ks), shape, jnp.float32)

    x = jax.random.normal(next(ks), (BATCH, SEQ, D_MODEL), jnp.float32)
    meta_tokens = jax.random.normal(next(ks), (N_META, D_MODEL), jnp.float32)
    u = jax.random.uniform(next(ks), (DEPTH, LRU_WIDTH), jnp.float32, 0.9, 0.999)
    p = u ** (1.0 / LRU_C)
    lru_a_param = jnp.log(p) - jnp.log1p(-p)
    return {
        'x': x,
        'meta_tokens': meta_tokens,
        'norm_ffn1': gain((DEPTH, D_MODEL)),
        'ffn1_w_gate': w((DEPTH, D_MODEL, D_FF), D_MODEL),
        'ffn1_w_up': w((DEPTH, D_MODEL, D_FF), D_MODEL),
        'ffn1_w_down': w((DEPTH, D_FF, D_MODEL), D_FF),
        'norm_mix': gain((DEPTH, D_MODEL)),
        'w_in': w((DEPTH, D_MODEL, N_IN), D_MODEL),
        'b_gate': small((DEPTH, 2 * D_MODEL)),
        'lambda_q1': small((DEPTH, HEAD_DIM), 0.1),
        'lambda_k1': small((DEPTH, HEAD_DIM), 0.1),
        'lambda_q2': small((DEPTH, HEAD_DIM), 0.1),
        'lambda_k2': small((DEPTH, HEAD_DIM), 0.1),
        'attn_subln': gain((DEPTH, V_DIM)),
        'conv_w': w((DEPTH, CONV_W, LRU_WIDTH), CONV_W),
        'conv_b': small((DEPTH, LRU_WIDTH)),
        'gate_x_w': w((DEPTH, LRU_BLOCKS, LRU_BLOCK_W, LRU_BLOCK_W), LRU_BLOCK_W),
        'gate_x_b': small((DEPTH, LRU_WIDTH)),
        'gate_a_w': w((DEPTH, LRU_BLOCKS, LRU_BLOCK_W, LRU_BLOCK_W), LRU_BLOCK_W),
        'gate_a_b': small((DEPTH, LRU_WIDTH)),
        'lru_a_param': lru_a_param,
        'w_branch_attn': w((DEPTH, ATTN_V, D_MODEL), ATTN_V),
        'w_branch_lru': w((DEPTH, LRU_WIDTH, D_MODEL), LRU_WIDTH),
        'w_out': w((DEPTH, D_MODEL, D_MODEL), D_MODEL),
        'norm_ffn2': gain((DEPTH, D_MODEL)),
        'ffn2_w_gate': w((DEPTH, D_MODEL, D_FF), D_MODEL),
        'ffn2_w_up': w((DEPTH, D_MODEL, D_FF), D_MODEL),
        'ffn2_w_down': w((DEPTH, D_FF, D_MODEL), D_FF),
        'final_norm': gain((D_MODEL,)),
    }


def reference(x, meta_tokens, norm_ffn1, ffn1_w_gate, ffn1_w_up, ffn1_w_down, norm_mix,
              w_in, b_gate, lambda_q1, lambda_k1, lambda_q2, lambda_k2, attn_subln,
              conv_w, conv_b, gate_x_w, gate_x_b, gate_a_w, gate_a_b, lru_a_param,
              w_branch_attn, w_branch_lru, w_out, norm_ffn2, ffn2_w_gate, ffn2_w_up,
              ffn2_w_down, final_norm):
    b = x.shape[0]
    meta = jnp.broadcast_to(meta_tokens.astype(x.dtype)[None], (b, N_META, D_MODEL))
    h = jnp.concatenate([meta, x], axis=1)
    cos, sin = rotary_tables(h.shape[1])
    for l in range(DEPTH):
        lam_init = 0.8 - 0.6 * math.exp(-0.3 * l)
        h = h + 0.5 * swiglu(rms_norm(h, norm_ffn1[l]), ffn1_w_gate[l], ffn1_w_up[l], ffn1_w_down[l])
        h = h + hybrid_mixer(rms_norm(h, norm_mix[l]), w_in[l], b_gate[l],
                             lambda_q1[l], lambda_k1[l], lambda_q2[l], lambda_k2[l], attn_subln[l],
                             conv_w[l], conv_b[l], gate_x_w[l], gate_x_b[l], gate_a_w[l], gate_a_b[l],
                             lru_a_param[l], w_branch_attn[l], w_branch_lru[l], w_out[l],
                             cos, sin, lam_init)
        h = h + 0.5 * swiglu(rms_norm(h, norm_ffn2[l]), ffn2_w_gate[l], ffn2_w_up[l], ffn2_w_down[l])
    return rms_norm(h, final_norm)[:, N_META:]
```

```python
import functools
import math

import jax
import jax.numpy as jnp
from jax import lax
from jax.experimental import pallas as pl
from jax.experimental.pallas import tpu as pltpu

N_META = 16
HEAD_DIM = 128
V_DIM = 2 * HEAD_DIM
ROT_DIM = HEAD_DIM // 4
ROPE_THETA = 500000.0
CONV_W = 4
LRU_C = 8.0
EPS = 1e-6

LANES = 128
SUBLANES = 8
SEQ_BLOCK = 384
LRU_SUPER = 640
VMEM_LIMIT_BYTES = 56 * 1024 * 1024
NEG_BIG = -1e30

BF16 = jnp.bfloat16
F32 = jnp.float32


def _pick(n, prefs):
    for p in prefs:
        if p <= n and n % p == 0:
            return p
    return n


def _params(semantics):
    return pltpu.CompilerParams(dimension_semantics=semantics, vmem_limit_bytes=VMEM_LIMIT_BYTES)


def _single(block_shape, index_map):
    return pl.BlockSpec(block_shape, index_map, pipeline_mode=pl.Buffered(1))


def _rms_norm_bf16(x, gain):
    ms = jnp.mean(x * x, axis=-1, keepdims=True)
    return (x * lax.rsqrt(ms + EPS) * gain).astype(BF16)


def _ffn_kernel(h_ref, gain_ref, wg_ref, wu_ref, wd_ref, o_ref, xn_ref):
    f = pl.program_id(1)

    @pl.when(f == 0)
    def _():
        x = h_ref[...]
        xn_ref[...] = _rms_norm_bf16(x, gain_ref[...])
        o_ref[...] = x

    xn = xn_ref[...]
    g = jnp.dot(xn, wg_ref[...], preferred_element_type=F32)
    u = jnp.dot(xn, wu_ref[...], preferred_element_type=F32)
    a = (0.5 * (g * jax.nn.sigmoid(g)) * u).astype(BF16)
    o_ref[...] += jnp.dot(a, wd_ref[...], preferred_element_type=F32)


def _ffn(h, gain, wg, wu, wd, layer):
    rows, d = h.shape
    ff = wg.shape[-1]
    tm = _pick(rows, (1056, 768, 512, 384, 256, 128))
    tf = _pick(ff, (512, 256, 128))
    return pl.pallas_call(
        _ffn_kernel,
        grid=(rows // tm, ff // tf),
        in_specs=[
            _single((tm, d), lambda i, f: (i, 0)),
            pl.BlockSpec((None, 1, d), lambda i, f: (layer, 0, 0)),
            pl.BlockSpec((None, d, tf), lambda i, f: (layer, 0, f)),
            pl.BlockSpec((None, d, tf), lambda i, f: (layer, 0, f)),
            pl.BlockSpec((None, tf, d), lambda i, f: (layer, f, 0)),
        ],
        out_specs=pl.BlockSpec((tm, d), lambda i, f: (i, 0)),
        out_shape=jax.ShapeDtypeStruct((rows, d), F32),
        scratch_shapes=[pltpu.VMEM((tm, d), BF16)],
        compiler_params=_params(("arbitrary", "arbitrary")),
    )(h, gain, wg, wu, wd)


def _proj_kernel(kind, h_ref, gain_ref, w_ref, *rest):
    o_ref, xn_ref = rest[-2], rest[-1]

    @pl.when(pl.program_id(1) == 0)
    def _():
        xn_ref[...] = _rms_norm_bf16(h_ref[...], gain_ref[...])

    z = jnp.dot(xn_ref[...], w_ref[...], preferred_element_type=F32)
    if kind == "rope":
        tab_ref = rest[0]
        tn = z.shape[1]
        reps = tn // LANES
        cos = jnp.concatenate([tab_ref[0]] * reps, axis=1)
        sin_lo = jnp.concatenate([tab_ref[1]] * reps, axis=1)
        sin_hi = jnp.concatenate([tab_ref[2]] * reps, axis=1)
        half = ROT_DIM // 2
        z = z * cos + pltpu.roll(z, half, 1) * sin_lo + pltpu.roll(z, tn - half, 1) * sin_hi
    elif kind == "gate":
        z = jax.nn.sigmoid(z + rest[0][...])
    o_ref[...] = z.astype(o_ref.dtype)


def _proj(h, gain, w, layer, col0, ncols, out_dtype, kind, extra=None, seq_len=None):
    rows, d = h.shape
    tm = _pick(seq_len, (1408, 768, 384, 128)) if seq_len else _pick(rows, (1408, 1056, 768, 384, 128))
    tn = _pick(ncols // 2 if kind == "rope" else ncols, (512, 256, 128))
    assert col0 % tn == 0
    cb0 = col0 // tn
    in_specs = [
        _single((tm, d), lambda i, n: (i, 0)),
        pl.BlockSpec((None, 1, d), lambda i, n: (layer, 0, 0)),
        pl.BlockSpec((None, d, tn), lambda i, n: (layer, 0, cb0 + n)),
    ]
    args = [h, gain, w]
    if kind == "rope":
        tiles_per_seq = seq_len // tm
        q_blocks = ncols // 2 // tn
        in_specs.append(pl.BlockSpec((None, 3, tm, LANES),
                                     lambda i, n: (n // q_blocks, 0, i % tiles_per_seq, 0)))
        args.append(extra)
    elif kind == "gate":
        in_specs.append(pl.BlockSpec((None, 1, tn), lambda i, n: (layer, 0, n)))
        args.append(extra)
    return pl.pallas_call(
        functools.partial(_proj_kernel, kind),
        grid=(rows // tm, ncols // tn),
        in_specs=in_specs,
        out_specs=pl.BlockSpec((tm, tn), lambda i, n: (i, n)),
        out_shape=jax.ShapeDtypeStruct((rows, ncols), out_dtype),
        scratch_shapes=[pltpu.VMEM((tm, d), BF16)],
        compiler_params=_params(("arbitrary", "arbitrary")),
    )(*args)


def _rope_tables(seq_len):
    half = ROT_DIM // 2
    inv_freq = ROPE_THETA ** (-jnp.arange(0, ROT_DIM, 2, dtype=F32) / ROT_DIM)
    ang = jnp.arange(seq_len, dtype=F32)[:, None] * inv_freq[None, :]
    cos, sin = jnp.cos(ang), jnp.sin(ang)
    ones = jnp.ones((seq_len, LANES - ROT_DIM), F32)
    zeros = jnp.zeros((seq_len, LANES - ROT_DIM), F32)
    zh = jnp.zeros((seq_len, half), F32)
    c = jnp.concatenate([cos, cos, ones], axis=1)
    s_lo = jnp.concatenate([zh, sin, zeros], axis=1)
    s_hi = jnp.concatenate([-sin, zh, zeros], axis=1)
    k_tab = jnp.stack([c, s_lo, s_hi])
    return jnp.stack([k_tab * (HEAD_DIM ** -0.5), k_tab])


def _attn_kernel(lam_init, q_ref, k_ref, v_ref, lq1_ref, lk1_ref, lq2_ref, lk2_ref, subln_ref,
                 o_ref, m_ref, l_ref, acc_ref):
    i = pl.program_id(2)
    tq = q_ref.shape[0]
    q = q_ref[...]
    qs = (q[:, :HEAD_DIM], q[:, HEAD_DIM:])

    m_ref[...] = jnp.full(m_ref.shape, NEG_BIG, F32)
    l_ref[...] = jnp.zeros(l_ref.shape, F32)
    acc_ref[...] = jnp.zeros(acc_ref.shape, F32)

    def step(j, masked):
        start = pl.multiple_of(j * tq, tq)
        kb = k_ref[pl.ds(start, tq), :]
        vb = v_ref[pl.ds(start, tq), :]
        for c in range(2):
            kc = kb[:, c * HEAD_DIM:(c + 1) * HEAD_DIM]
            s = lax.dot_general(qs[c], kc, (((1,), (1,)), ((), ())), preferred_element_type=F32)
            if masked:
                row = lax.broadcasted_iota(jnp.int32, s.shape, 0)
                col = lax.broadcasted_iota(jnp.int32, s.shape, 1)
                s = jnp.where(col <= row, s, NEG_BIG)
            m_old = m_ref[c]
            m_new = jnp.maximum(m_old, jnp.max(s, axis=-1, keepdims=True))
            alpha = jnp.exp(m_old - m_new)
            p = jnp.exp(s - m_new)
            l_ref[c] = alpha * l_ref[c] + jnp.sum(p, axis=-1, keepdims=True)
            acc_ref[c] = alpha * acc_ref[c] + jnp.dot(p.astype(BF16), vb, preferred_element_type=F32)
            m_ref[c] = m_new

    def body(j, carry):
        step(j, False)
        return carry

    lax.fori_loop(0, i, body, 0)
    step(i, True)

    lam = (jnp.exp(jnp.sum(lq1_ref[...] * lk1_ref[...], keepdims=True))
           - jnp.exp(jnp.sum(lq2_ref[...] * lk2_ref[...], keepdims=True)) + lam_init)
    o = acc_ref[0] / l_ref[0] - lam * (acc_ref[1] / l_ref[1])
    o = o * lax.rsqrt(jnp.mean(o * o, axis=-1, keepdims=True) + EPS)
    o = o * subln_ref[...] * (1.0 - lam_init)
    o_ref[...] = o.astype(o_ref.dtype)


def _attention(qk, v, lq1, lk1, lq2, lk2, subln, layer, lam_init, batch, seq_len):
    rows = qk.shape[0]
    n_heads = v.shape[1] // V_DIM
    tq = SEQ_BLOCK
    nq = seq_len // tq
    lam_spec = pl.BlockSpec((None, 1, HEAD_DIM), lambda b, h, i: (layer, 0, 0))
    return pl.pallas_call(
        functools.partial(_attn_kernel, lam_init),
        grid=(batch, n_heads, nq),
        in_specs=[
            pl.BlockSpec((tq, V_DIM), lambda b, h, i: (b * nq + i, h)),
            pl.BlockSpec((seq_len, V_DIM), lambda b, h, i: (b, n_heads + h)),
            pl.BlockSpec((seq_len, V_DIM), lambda b, h, i: (b, h)),
            lam_spec, lam_spec, lam_spec, lam_spec,
            pl.BlockSpec((None, 1, V_DIM), lambda b, h, i: (layer, 0, 0)),
        ],
        out_specs=pl.BlockSpec((tq, V_DIM), lambda b, h, i: (b * nq + i, h)),
        out_shape=jax.ShapeDtypeStruct((rows, n_heads * V_DIM), BF16),
        scratch_shapes=[
            pltpu.VMEM((2, tq, 1), F32),
            pltpu.VMEM((2, tq, 1), F32),
            pltpu.VMEM((2, tq, V_DIM), F32),
        ],
        compiler_params=_params(("arbitrary", "arbitrary", "arbitrary")),
    )(qk, qk, v, lq1, lk1, lq2, lk2, subln)


def _lru_kernel(xr_ref, yr_ref, p_ref, w_ref, o_ref, xbuf_ref, carry_ref, a_ref, b_ref, h_ref):
    t = pl.program_id(2)
    tt, cw = xr_ref.shape
    halo = SUBLANES

    @pl.when(t == 0)
    def _():
        xbuf_ref[0:halo, :] = jnp.zeros((halo, cw), F32)
        carry_ref[...] = jnp.zeros(carry_ref.shape, F32)

    x = xr_ref[...]
    xbuf_ref[halo:halo + tt, :] = x
    prm = p_ref[...]
    conv_b, ga_b, gx_b, a_par = prm[0:1], prm[1:2], prm[2:3], prm[3:4]
    xc = conv_b + x * prm[4:5]
    for j in range(1, CONV_W):
        xc = xc + xbuf_ref[halo - j:halo - j + tt, :] * prm[4 + j:5 + j]
    xbuf_ref[0:halo, :] = x[tt - halo:tt, :]

    gz = jnp.dot(xc.astype(BF16), w_ref[...], preferred_element_type=F32)
    r = jax.nn.sigmoid(gz[:, :cw] + ga_b)
    ig = jax.nn.sigmoid(gz[:, cw:] + gx_b)
    log_a = LRU_C * r * jax.nn.log_sigmoid(a_par)
    a = jnp.exp(log_a)
    bx = jnp.sqrt(-jnp.tanh(log_a) * (a * a + 1.0)) * (ig * xc)

    row8 = lax.broadcasted_iota(jnp.int32, (tt, cw), 0) & (SUBLANES - 1)
    for s in (1, 2, 4):
        keep = row8 >= s
        bx = jnp.where(keep, a * pltpu.roll(bx, s, 0) + bx, bx)
        a = jnp.where(keep, a * pltpu.roll(a, s, 0), a)
    a_ref[...] = a
    b_ref[...] = bx

    carry = carry_ref[...]
    for g in range(tt // SUBLANES):
        sl = slice(g * SUBLANES, (g + 1) * SUBLANES)
        hg = a_ref[sl, :] * carry + b_ref[sl, :]
        h_ref[sl, :] = hg
        carry = hg[SUBLANES - 1:SUBLANES, :]
    carry_ref[...] = carry

    o_ref[...] = (h_ref[...] * jax.nn.gelu(yr_ref[...])).astype(o_ref.dtype)


def _lru(z_lru, prm, w_gate, layer, batch, seq_len):
    rows = z_lru.shape[0]
    width = z_lru.shape[1] // 2
    cw = LRU_SUPER
    ns = width // cw
    tt = SEQ_BLOCK
    nt = seq_len // tt
    return pl.pallas_call(
        _lru_kernel,
        grid=(batch, ns, nt),
        in_specs=[
            pl.BlockSpec((tt, cw), lambda b, s, t: (b * nt + t, s)),
            pl.BlockSpec((tt, cw), lambda b, s, t: (b * nt + t, ns + s)),
            pl.BlockSpec((None, 2 * CONV_W, cw), lambda b, s, t: (layer, 0, s)),
            pl.BlockSpec((None, None, cw, 2 * cw), lambda b, s, t: (layer, s, 0, 0)),
        ],
        out_specs=pl.BlockSpec((tt, cw), lambda b, s, t: (b * nt + t, s)),
        out_shape=jax.ShapeDtypeStruct((rows, width), BF16),
        scratch_shapes=[
            pltpu.VMEM((tt + SUBLANES, cw), F32),
            pltpu.VMEM((1, cw), F32),
            pltpu.VMEM((tt, cw), F32),
            pltpu.VMEM((tt, cw), F32),
            pltpu.VMEM((tt, cw), F32),
        ],
        compiler_params=_params(("arbitrary", "arbitrary", "arbitrary")),
    )(z_lru, z_lru, prm, w_gate)


def _lru_gate_weights(ga_w, gx_w):
    depth, nb, bw, _ = ga_w.shape
    per = LRU_SUPER // bw
    ns = nb // per

    def dense(w):
        w = w.reshape(depth, ns, per, bw, bw)
        eye = jnp.eye(per, dtype=w.dtype)
        full = jnp.einsum("lspkj,pq->lspkqj", w, eye)
        return full.reshape(depth, ns, per * bw, per * bw)

    return jnp.concatenate([dense(ga_w), dense(gx_w)], axis=-1).astype(BF16)


def _mix_kernel(oa_ref, ol_ref, wa_ref, wl_ref, ga_ref, gl_ref, o_ref):
    ya = jnp.dot(oa_ref[...], wa_ref[...], preferred_element_type=F32)
    yl = jnp.dot(ol_ref[...], wl_ref[...], preferred_element_type=F32)
    o_ref[...] = (ga_ref[...] * ya + gl_ref[...] * yl).astype(o_ref.dtype)


def _mix(oa, ol, w_ba, w_bl, gates, layer):
    rows, da = oa.shape
    dl = ol.shape[1]
    d = w_ba.shape[-1]
    tm = _pick(rows, (1056, 768, 384, 128))
    tn = _pick(d, (512, 256, 128))
    nb = d // tn
    return pl.pallas_call(
        _mix_kernel,
        grid=(rows // tm, nb),
        in_specs=[
            _single((tm, da), lambda i, n: (i, 0)),
            _single((tm, dl), lambda i, n: (i, 0)),
            pl.BlockSpec((None, da, tn), lambda i, n: (layer, 0, n)),
            pl.BlockSpec((None, dl, tn), lambda i, n: (layer, 0, n)),
            pl.BlockSpec((tm, tn), lambda i, n: (i, n)),
            pl.BlockSpec((tm, tn), lambda i, n: (i, nb + n)),
        ],
        out_specs=pl.BlockSpec((tm, tn), lambda i, n: (i, n)),
        out_shape=jax.ShapeDtypeStruct((rows, d), BF16),
        compiler_params=_params(("arbitrary", "arbitrary")),
    )(oa, ol, w_ba, w_bl, gates, gates)


def _out_kernel(y_ref, w_ref, h_ref, o_ref):
    o_ref[...] = h_ref[...] + jnp.dot(y_ref[...], w_ref[...], preferred_element_type=F32)


def _out_proj(y, w_o, h, layer):
    rows, d = h.shape
    tm = _pick(rows, (1056, 768, 384, 128))
    tn = _pick(d, (512, 256, 128))
    return pl.pallas_call(
        _out_kernel,
        grid=(rows // tm, d // tn),
        in_specs=[
            _single((tm, y.shape[1]), lambda i, n: (i, 0)),
            pl.BlockSpec((None, y.shape[1], tn), lambda i, n: (layer, 0, n)),
            pl.BlockSpec((tm, tn), lambda i, n: (i, n)),
        ],
        out_specs=pl.BlockSpec((tm, tn), lambda i, n: (i, n)),
        out_shape=jax.ShapeDtypeStruct((rows, d), F32),
        compiler_params=_params(("arbitrary", "arbitrary")),
    )(y, w_o, h)


def _final_norm_kernel(h_ref, g_ref, o_ref):
    x = h_ref[...]
    ms = jnp.mean(x * x, axis=-1, keepdims=True)
    o_ref[...] = x * lax.rsqrt(ms + EPS) * g_ref[...]


def _final_norm(h, gain):
    rows, d = h.shape
    tm = _pick(rows, (1056, 768, 384, 128))
    return pl.pallas_call(
        _final_norm_kernel,
        grid=(rows // tm,),
        in_specs=[pl.BlockSpec((tm, d), lambda i: (i, 0)), pl.BlockSpec((1, d), lambda i: (0, 0))],
        out_specs=pl.BlockSpec((tm, d), lambda i: (i, 0)),
        out_shape=jax.ShapeDtypeStruct((rows, d), F32),
        compiler_params=_params(("arbitrary",)),
    )(h, gain)


def kernel(x, meta_tokens, norm_ffn1, ffn1_w_gate, ffn1_w_up, ffn1_w_down, norm_mix, w_in, b_gate, lambda_q1, lambda_k1, lambda_q2, lambda_k2, attn_subln, conv_w, conv_b, gate_x_w, gate_x_b, gate_a_w, gate_a_b, lru_a_param, w_branch_attn, w_branch_lru, w_out, norm_ffn2, ffn2_w_gate, ffn2_w_up, ffn2_w_down, final_norm):
    batch, seq, d = x.shape
    depth = w_in.shape[0]
    attn_v = w_branch_attn.shape[1]
    attn_qk = attn_v
    lru_w = w_branch_lru.shape[1]
    seq_real = N_META + seq
    tp = -(-seq_real // SEQ_BLOCK) * SEQ_BLOCK
    rows = batch * tp

    meta = jnp.broadcast_to(meta_tokens.astype(x.dtype)[None], (batch, N_META, d))
    h = jnp.concatenate([meta, x, jnp.zeros((batch, tp - seq_real, d), x.dtype)], axis=1)
    h = h.reshape(rows, d)

    def vec(p):
        return p[:, None, :]

    bf = lambda w: w.astype(BF16)
    f1g, f1u, f1d = bf(ffn1_w_gate), bf(ffn1_w_up), bf(ffn1_w_down)
    f2g, f2u, f2d = bf(ffn2_w_gate), bf(ffn2_w_up), bf(ffn2_w_down)
    w_in_b, w_ba, w_bl, w_o = bf(w_in), bf(w_branch_attn), bf(w_branch_lru), bf(w_out)
    w_lru_gate = _lru_gate_weights(gate_a_w, gate_x_w)
    lru_prm = jnp.concatenate([conv_b[:, None], gate_a_b[:, None], gate_x_b[:, None],
                               lru_a_param[:, None], conv_w], axis=1)
    rope = _rope_tables(tp)
    n1, nm, n2 = vec(norm_ffn1), vec(norm_mix), vec(norm_ffn2)
    bg = vec(b_gate)
    lq1, lk1, lq2, lk2 = vec(lambda_q1), vec(lambda_k1), vec(lambda_q2), vec(lambda_k2)
    subln = vec(attn_subln)

    c_v = 2 * attn_qk
    c_lru = c_v + attn_v
    c_gate = c_lru + 2 * lru_w
    for l in range(depth):
        lam_init = 0.8 - 0.6 * math.exp(-0.3 * l)
        h = _ffn(h, n1, f1g, f1u, f1d, l)
        qk = _proj(h, nm, w_in_b, l, 0, 2 * attn_qk, BF16, "rope", rope, seq_len=tp)
        v = _proj(h, nm, w_in_b, l, c_v, attn_v, BF16, "plain")
        z_lru = _proj(h, nm, w_in_b, l, c_lru, 2 * lru_w, F32, "plain")
        gates = _proj(h, nm, w_in_b, l, c_gate, 2 * d, F32, "gate", bg)
        o_attn = _attention(qk, v, lq1, lk1, lq2, lk2, subln, l, lam_init, batch, tp)
        o_lru = _lru(z_lru, lru_prm, w_lru_gate, l, batch, tp)
        y = _mix(o_attn, o_lru, w_ba, w_bl, gates, l)
        h = _out_proj(y, w_o, h, l)
        h = _ffn(h, n2, f2g, f2u, f2d, l)
    out = _final_norm(h, final_norm[None, :])
    return out.reshape(batch, tp, d)[:, N_META:seq_real]
```

```python
import functools
import math

import jax
import jax.numpy as jnp
from jax import lax
from jax.experimental import pallas as pl
from jax.experimental.pallas import tpu as pltpu

N_META = 16
HEAD_DIM = 128
V_DIM = 2 * HEAD_DIM
ROT_DIM = HEAD_DIM // 4
ROPE_THETA = 500000.0
CONV_W = 4
LRU_C = 8.0
EPS = 1e-6

LANES = 128
SUBLANES = 8
SEQ_BLOCK = 384
ATTN_CHUNK = 2 * SEQ_BLOCK
LRU_SUPER = 640
VMEM_LIMIT_BYTES = 56 * 1024 * 1024
NEG_BIG = -1e30

BF16 = jnp.bfloat16
F32 = jnp.float32


def _pick(n, prefs):
    for p in prefs:
        if p <= n and n % p == 0:
            return p
    return n


def _params(semantics):
    return pltpu.CompilerParams(dimension_semantics=semantics, vmem_limit_bytes=VMEM_LIMIT_BYTES)


def _single(block_shape, index_map):
    return pl.BlockSpec(block_shape, index_map, pipeline_mode=pl.Buffered(1))


def _sigmoid(x):
    return 0.5 * jnp.tanh(0.5 * x) + 0.5


def _rms_norm_bf16(x, gain):
    ms = jnp.mean(x * x, axis=-1, keepdims=True)
    return (x * lax.rsqrt(ms + EPS) * gain).astype(BF16)


def _ffn_kernel(h_ref, gain_ref, wg_ref, wu_ref, wd_ref, o_ref, xn_ref):
    f = pl.program_id(1)

    @pl.when(f == 0)
    def _():
        x = h_ref[...]
        xn_ref[...] = _rms_norm_bf16(x, gain_ref[...])
        o_ref[...] = x

    xn = xn_ref[...]
    g = jnp.dot(xn, wg_ref[...], preferred_element_type=F32)
    u = jnp.dot(xn, wu_ref[...], preferred_element_type=F32)
    a = (0.5 * (g * _sigmoid(g)) * u).astype(BF16)
    o_ref[...] += jnp.dot(a, wd_ref[...], preferred_element_type=F32)


def _ffn(h, gain, wg, wu, wd, layer):
    rows, d = h.shape
    ff = wg.shape[-1]
    tm = _pick(rows, (1056, 768, 512, 384, 256, 128))
    tf = _pick(ff, (512, 256, 128))
    return pl.pallas_call(
        _ffn_kernel,
        grid=(rows // tm, ff // tf),
        in_specs=[
            _single((tm, d), lambda i, f: (i, 0)),
            pl.BlockSpec((None, 1, d), lambda i, f: (layer, 0, 0)),
            pl.BlockSpec((None, d, tf), lambda i, f: (layer, 0, f)),
            pl.BlockSpec((None, d, tf), lambda i, f: (layer, 0, f)),
            pl.BlockSpec((None, tf, d), lambda i, f: (layer, f, 0)),
        ],
        out_specs=pl.BlockSpec((tm, d), lambda i, f: (i, 0)),
        out_shape=jax.ShapeDtypeStruct((rows, d), F32),
        scratch_shapes=[pltpu.VMEM((tm, d), BF16)],
        compiler_params=_params(("arbitrary", "arbitrary")),
        name="ffn",
    )(h, gain, wg, wu, wd)


def _norm_kernel(h_ref, gain_ref, o_ref):
    o_ref[...] = _rms_norm_bf16(h_ref[...], gain_ref[...])


def _norm(h, gain, layer):
    rows, d = h.shape
    tm = _pick(rows, (1056, 768, 384, 128))
    return pl.pallas_call(
        _norm_kernel,
        grid=(rows // tm,),
        in_specs=[pl.BlockSpec((tm, d), lambda i: (i, 0)),
                  pl.BlockSpec((None, 1, d), lambda i: (layer, 0, 0))],
        out_specs=pl.BlockSpec((tm, d), lambda i: (i, 0)),
        out_shape=jax.ShapeDtypeStruct((rows, d), BF16),
        compiler_params=_params(("arbitrary",)),
        name="mix_norm",
    )(h, gain)


def _proj_kernel(kind, xn_ref, w_ref, *rest):
    o_ref = rest[-1]
    tm, tn = o_ref.shape
    chunks = {"rope": 4, "gate": 2, "plain": 1}[kind]
    cm = tm // chunks
    for r in range(chunks):
        rows = slice(r * cm, (r + 1) * cm)
        z = jnp.dot(xn_ref[rows, :], w_ref[...], preferred_element_type=F32)
        if kind == "rope":
            tab_ref = rest[0]
            reps = tn // LANES
            cos = jnp.concatenate([tab_ref[0, rows, :]] * reps, axis=1)
            sin_lo = jnp.concatenate([tab_ref[1, rows, :]] * reps, axis=1)
            sin_hi = jnp.concatenate([tab_ref[2, rows, :]] * reps, axis=1)
            half = ROT_DIM // 2
            z = z * cos + pltpu.roll(z, half, 1) * sin_lo + pltpu.roll(z, tn - half, 1) * sin_hi
        elif kind == "gate":
            z = _sigmoid(z + rest[0][...])
        o_ref[rows, :] = z.astype(o_ref.dtype)


def _proj(xn, w, layer, col0, ncols, out_dtype, kind, extra=None, seq_len=None):
    rows, d = xn.shape
    tm = _pick(seq_len, (2112, 1408, 768, 384, 128)) if seq_len else _pick(rows, (2112, 1408, 768, 384, 128))
    tn = _pick(ncols // 2 if kind == "rope" else ncols, (512, 256, 128))
    assert col0 % tn == 0
    cb0 = col0 // tn
    in_specs = [
        pl.BlockSpec((tm, d), lambda i, n: (i, 0)),
        pl.BlockSpec((None, d, tn), lambda i, n: (layer, 0, cb0 + n)),
    ]
    args = [xn, w]
    if kind == "rope":
        tiles_per_seq = seq_len // tm
        q_blocks = ncols // 2 // tn
        in_specs.append(pl.BlockSpec((None, 3, tm, LANES),
                                     lambda i, n: (n // q_blocks, 0, i % tiles_per_seq, 0)))
        args.append(extra)
    elif kind == "gate":
        in_specs.append(pl.BlockSpec((None, 1, tn), lambda i, n: (layer, 0, n)))
        args.append(extra)
    return pl.pallas_call(
        functools.partial(_proj_kernel, kind),
        grid=(rows // tm, ncols // tn),
        in_specs=in_specs,
        out_specs=pl.BlockSpec((tm, tn), lambda i, n: (i, n)),
        out_shape=jax.ShapeDtypeStruct((rows, ncols), out_dtype),
        compiler_params=_params(("arbitrary", "arbitrary")),
        name="proj_" + kind,
    )(*args)


def _rope_tables(seq_len):
    half = ROT_DIM // 2
    inv_freq = ROPE_THETA ** (-jnp.arange(0, ROT_DIM, 2, dtype=F32) / ROT_DIM)
    ang = jnp.arange(seq_len, dtype=F32)[:, None] * inv_freq[None, :]
    cos, sin = jnp.cos(ang), jnp.sin(ang)
    ones = jnp.ones((seq_len, LANES - ROT_DIM), F32)
    zeros = jnp.zeros((seq_len, LANES - ROT_DIM), F32)
    zh = jnp.zeros((seq_len, half), F32)
    c = jnp.concatenate([cos, cos, ones], axis=1)
    s_lo = jnp.concatenate([zh, sin, zeros], axis=1)
    s_hi = jnp.concatenate([-sin, zh, zeros], axis=1)
    k_tab = jnp.stack([c, s_lo, s_hi])
    return jnp.stack([k_tab * ((HEAD_DIM ** -0.5) * math.log2(math.e)), k_tab])


def _attn_stages(k_ref, v_ref, qs, m_ref, l_ref, acc_ref, p_refs, a_refs):
    def score_stage(start, width, masked, slot):
        kb = k_ref[pl.ds(start, width), :]
        for c in range(2):
            kc = kb[:, c * HEAD_DIM:(c + 1) * HEAD_DIM]
            s = lax.dot_general(qs[c], kc, (((1,), (1,)), ((), ())), preferred_element_type=F32)
            if masked:
                row = lax.broadcasted_iota(jnp.int32, s.shape, 0)
                col = lax.broadcasted_iota(jnp.int32, s.shape, 1)
                s = jnp.where(col <= row, s, NEG_BIG)
            m_old = m_ref[c]
            m_new = jnp.maximum(m_old, jnp.max(s, axis=-1, keepdims=True))
            alpha = jnp.exp2(m_old - m_new)
            p = jnp.exp2(s - pltpu.repeat(m_new, width // LANES, 1))
            l_ref[c] = alpha * l_ref[c] + jnp.sum(p, axis=-1, keepdims=True)
            m_ref[c] = m_new
            a_refs[slot][c] = alpha
            p_refs[slot][c, :, 0:width] = p.astype(BF16)

    def value_stage(start, width, slot):
        vb = v_ref[pl.ds(start, width), :]
        for c in range(2):
            pv = jnp.dot(p_refs[slot][c, :, 0:width], vb, preferred_element_type=F32)
            acc_ref[c] = pltpu.repeat(a_refs[slot][c], V_DIM // LANES, 1) * acc_ref[c] + pv

    return score_stage, value_stage


def _attn_init(m_ref, l_ref, acc_ref):
    m_ref[...] = jnp.full(m_ref.shape, NEG_BIG, F32)
    l_ref[...] = jnp.zeros(l_ref.shape, F32)
    acc_ref[...] = jnp.zeros(acc_ref.shape, F32)


def _attn_finish(lam_init, lq1_ref, lk1_ref, lq2_ref, lk2_ref, subln_ref, o_ref, l_ref, acc_ref):
    lam = (jnp.exp(jnp.sum(lq1_ref[...] * lk1_ref[...], keepdims=True))
           - jnp.exp(jnp.sum(lq2_ref[...] * lk2_ref[...], keepdims=True)) + lam_init)
    reps = V_DIM // LANES
    o = (acc_ref[0] * pltpu.repeat(1.0 / l_ref[0], reps, 1)
         - lam * (acc_ref[1] * pltpu.repeat(1.0 / l_ref[1], reps, 1)))
    o = o * lax.rsqrt(jnp.mean(o * o, axis=-1, keepdims=True) + EPS)
    o = o * subln_ref[...] * (1.0 - lam_init)
    o_ref[...] = o.astype(o_ref.dtype)


def _attn_main_kernel(lam_init, q_ref, k_ref, v_ref, lq1_ref, lk1_ref, lq2_ref, lk2_ref, subln_ref,
                      o_ref, m_ref, l_ref, acc_ref, p0_ref, p1_ref, a0_ref, a1_ref):
    i = pl.program_id(2)
    tq = q_ref.shape[0]
    q = q_ref[...]
    score_stage, value_stage = _attn_stages(k_ref, v_ref, (q[:, :HEAD_DIM], q[:, HEAD_DIM:]),
                                            m_ref, l_ref, acc_ref, (p0_ref, p1_ref), (a0_ref, a1_ref))
    _attn_init(m_ref, l_ref, acc_ref)
    score_stage(pl.multiple_of(i * tq, tq), tq, True, 0)

    def chunk_start(n):
        return pl.multiple_of(jnp.where(n == 0, i, n - 1) * tq, tq)

    def body(n, carry):
        for slot in range(2):
            @pl.when(n % 2 == slot)
            def _():
                score_stage(chunk_start(n), tq, False, slot)
                value_stage(chunk_start(n - 1), tq, 1 - slot)
        return carry

    lax.fori_loop(1, i + 1, body, 0)
    for slot in range(2):
        @pl.when(i % 2 == slot)
        def _():
            value_stage(chunk_start(i), tq, slot)
    _attn_finish(lam_init, lq1_ref, lk1_ref, lq2_ref, lk2_ref, subln_ref, o_ref, l_ref, acc_ref)


def _attn_tail_kernel(lam_init, tk, q_ref, k_ref, v_ref, lq1_ref, lk1_ref, lq2_ref, lk2_ref, subln_ref,
                      o_in_ref, o_ref, m_ref, l_ref, acc_ref, p0_ref, p1_ref, a0_ref, a1_ref):
    del o_in_ref
    tq = q_ref.shape[0]
    n_full = (k_ref.shape[0] - tq) // tk
    q = q_ref[...]
    score_stage, value_stage = _attn_stages(k_ref, v_ref, (q[:, :HEAD_DIM], q[:, HEAD_DIM:]),
                                            m_ref, l_ref, acc_ref, (p0_ref, p1_ref), (a0_ref, a1_ref))
    _attn_init(m_ref, l_ref, acc_ref)
    chunks = [(n_full * tk, tq)] + [(j * tk, tk) for j in range(n_full)]
    score_stage(chunks[0][0], chunks[0][1], True, 0)
    for n in range(1, len(chunks)):
        score_stage(chunks[n][0], chunks[n][1], False, n % 2)
        value_stage(chunks[n - 1][0], chunks[n - 1][1], (n - 1) % 2)
    value_stage(chunks[-1][0], chunks[-1][1], (len(chunks) - 1) % 2)
    _attn_finish(lam_init, lq1_ref, lk1_ref, lq2_ref, lk2_ref, subln_ref, o_ref, l_ref, acc_ref)


def _attn_scratch(tq, tk):
    return [
        pltpu.VMEM((2, tq, LANES), F32),
        pltpu.VMEM((2, tq, LANES), F32),
        pltpu.VMEM((2, tq, V_DIM), F32),
        pltpu.VMEM((2, tq, tk), BF16),
        pltpu.VMEM((2, tq, tk), BF16),
        pltpu.VMEM((2, tq, LANES), F32),
        pltpu.VMEM((2, tq, LANES), F32),
    ]


def _attention(qk, v, lq1, lk1, lq2, lk2, subln, layer, lam_init, batch, seq_len):
    n_heads = v.shape[1] // V_DIM
    tk = ATTN_CHUNK
    nq = seq_len // tk
    rem = seq_len - nq * tk
    qk3 = qk.reshape(batch, seq_len, qk.shape[1])
    v3 = v.reshape(batch, seq_len, v.shape[1])
    out_shape = jax.ShapeDtypeStruct((batch, seq_len, n_heads * V_DIM), BF16)

    def kv_param_specs(grid_rank):
        def at(*idx):
            return (lambda b, h, i: idx_of(b, h, idx)) if grid_rank == 3 else (lambda b, h: idx_of(b, h, idx))

        def idx_of(b, h, idx):
            return tuple(f(b, h) if callable(f) else f for f in idx)

        lam_spec = pl.BlockSpec((None, 1, HEAD_DIM), at(layer, 0, 0))
        return [pl.BlockSpec((None, seq_len, V_DIM), at(lambda b, h: b, 0, lambda b, h: n_heads + h)),
                pl.BlockSpec((None, seq_len, V_DIM), at(lambda b, h: b, 0, lambda b, h: h)),
                lam_spec, lam_spec, lam_spec, lam_spec,
                pl.BlockSpec((None, 1, V_DIM), at(layer, 0, 0))]

    out = pl.pallas_call(
        functools.partial(_attn_main_kernel, lam_init),
        grid=(batch, n_heads, nq),
        in_specs=[pl.BlockSpec((None, tk, V_DIM), lambda b, h, i: (b, i, h))] + kv_param_specs(3),
        out_specs=pl.BlockSpec((None, tk, V_DIM), lambda b, h, i: (b, i, h)),
        out_shape=out_shape,
        scratch_shapes=_attn_scratch(tk, tk),
        compiler_params=_params(("arbitrary", "arbitrary", "arbitrary")),
        name="attn_main",
    )(qk3, qk3, v3, lq1, lk1, lq2, lk2, subln)
    if rem:
        assert seq_len % rem == 0
        last = seq_len // rem - 1
        out = pl.pallas_call(
            functools.partial(_attn_tail_kernel, lam_init, tk),
            grid=(batch, n_heads),
            in_specs=([pl.BlockSpec((None, rem, V_DIM), lambda b, h: (b, last, h))] + kv_param_specs(2)
                      + [pl.BlockSpec(memory_space=pl.ANY)]),
            out_specs=pl.BlockSpec((None, rem, V_DIM), lambda b, h: (b, last, h)),
            out_shape=out_shape,
            scratch_shapes=_attn_scratch(rem, tk),
            input_output_aliases={8: 0},
            compiler_params=_params(("arbitrary", "arbitrary")),
            name="attn_tail",
        )(qk3, qk3, v3, lq1, lk1, lq2, lk2, subln, out)
    return out.reshape(batch * seq_len, n_heads * V_DIM)


def _lru_kernel(xr_ref, yr_ref, p_ref, w_ref, o_ref, xbuf_ref, carry_ref, a_ref, b_ref, h_ref):
    t = pl.program_id(2)
    tt, cw = xr_ref.shape
    halo = SUBLANES

    @pl.when(t == 0)
    def _():
        xbuf_ref[0:halo, :] = jnp.zeros((halo, cw), F32)
        carry_ref[...] = jnp.zeros(carry_ref.shape, F32)

    x = xr_ref[...]
    xbuf_ref[halo:halo + tt, :] = x
    prm = p_ref[...]
    conv_b, ga_b, gx_b, a_par = prm[0:1], prm[1:2], prm[2:3], prm[3:4]
    xc = conv_b + x * prm[4:5]
    for j in range(1, CONV_W):
        xc = xc + xbuf_ref[halo - j:halo - j + tt, :] * prm[4 + j:5 + j]
    xbuf_ref[0:halo, :] = x[tt - halo:tt, :]

    gz = jnp.dot(xc.astype(BF16), w_ref[...], preferred_element_type=F32)
    r = _sigmoid(gz[:, :cw] + ga_b)
    ig = _sigmoid(gz[:, cw:] + gx_b)
    log_a = LRU_C * r * jax.nn.log_sigmoid(a_par)
    a = jnp.exp(log_a)
    bx = jnp.sqrt(-jnp.tanh(log_a) * (a * a + 1.0)) * (ig * xc)

    row8 = lax.broadcasted_iota(jnp.int32, (tt, cw), 0) & (SUBLANES - 1)
    for s in (1, 2, 4):
        keep = row8 >= s
        bx = jnp.where(keep, a * pltpu.roll(bx, s, 0) + bx, bx)
        a = jnp.where(keep, a * pltpu.roll(a, s, 0), a)
    a_ref[...] = a
    b_ref[...] = bx

    carry = carry_ref[...]
    for g in range(tt // SUBLANES):
        sl = slice(g * SUBLANES, (g + 1) * SUBLANES)
        hg = a_ref[sl, :] * carry + b_ref[sl, :]
        h_ref[sl, :] = hg
        carry = hg[SUBLANES - 1:SUBLANES, :]
    carry_ref[...] = carry

    o_ref[...] = (h_ref[...] * jax.nn.gelu(yr_ref[...])).astype(o_ref.dtype)


def _lru(z_lru, prm, w_gate, layer, batch, seq_len):
    rows = z_lru.shape[0]
    width = z_lru.shape[1] // 2
    cw = LRU_SUPER
    ns = width // cw
    tt = SEQ_BLOCK
    nt = seq_len // tt
    return pl.pallas_call(
        _lru_kernel,
        grid=(batch, ns, nt),
        in_specs=[
            pl.BlockSpec((tt, cw), lambda b, s, t: (b * nt + t, s)),
            pl.BlockSpec((tt, cw), lambda b, s, t: (b * nt + t, ns + s)),
            pl.BlockSpec((None, 2 * CONV_W, cw), lambda b, s, t: (layer, 0, s)),
            pl.BlockSpec((None, None, cw, 2 * cw), lambda b, s, t: (layer, s, 0, 0)),
        ],
        out_specs=pl.BlockSpec((tt, cw), lambda b, s, t: (b * nt + t, s)),
        out_shape=jax.ShapeDtypeStruct((rows, width), BF16),
        scratch_shapes=[
            pltpu.VMEM((tt + SUBLANES, cw), F32),
            pltpu.VMEM((1, cw), F32),
            pltpu.VMEM((tt, cw), F32),
            pltpu.VMEM((tt, cw), F32),
            pltpu.VMEM((tt, cw), F32),
        ],
        compiler_params=_params(("arbitrary", "arbitrary", "arbitrary")),
        name="rglru",
    )(z_lru, z_lru, prm, w_gate)


def _lru_gate_weights(ga_w, gx_w):
    depth, nb, bw, _ = ga_w.shape
    per = LRU_SUPER // bw
    ns = nb // per

    def dense(w):
        w = w.reshape(depth, ns, per, bw, bw)
        eye = jnp.eye(per, dtype=w.dtype)
        full = jnp.einsum("lspkj,pq->lspkqj", w, eye)
        return full.reshape(depth, ns, per * bw, per * bw)

    return jnp.concatenate([dense(ga_w), dense(gx_w)], axis=-1).astype(BF16)


def _mix_kernel(oa_ref, ol_ref, wa_ref, wl_ref, ga_ref, gl_ref, o_ref):
    ya = jnp.dot(oa_ref[...], wa_ref[...], preferred_element_type=F32)
    yl = jnp.dot(ol_ref[...], wl_ref[...], preferred_element_type=F32)
    o_ref[...] = (ga_ref[...] * ya + gl_ref[...] * yl).astype(o_ref.dtype)


def _mix(oa, ol, w_ba, w_bl, gates, layer):
    rows, da = oa.shape
    dl = ol.shape[1]
    d = w_ba.shape[-1]
    tm = _pick(rows, (1056, 768, 384, 128))
    tn = _pick(d, (512, 256, 128))
    nb = d // tn
    return pl.pallas_call(
        _mix_kernel,
        grid=(rows // tm, nb),
        in_specs=[
            _single((tm, da), lambda i, n: (i, 0)),
            _single((tm, dl), lambda i, n: (i, 0)),
            pl.BlockSpec((None, da, tn), lambda i, n: (layer, 0, n)),
            pl.BlockSpec((None, dl, tn), lambda i, n: (layer, 0, n)),
            pl.BlockSpec((tm, tn), lambda i, n: (i, n)),
            pl.BlockSpec((tm, tn), lambda i, n: (i, nb + n)),
        ],
        out_specs=pl.BlockSpec((tm, tn), lambda i, n: (i, n)),
        out_shape=jax.ShapeDtypeStruct((rows, d), BF16),
        compiler_params=_params(("arbitrary", "arbitrary")),
        name="branch_mix",
    )(oa, ol, w_ba, w_bl, gates, gates)


def _out_kernel(y_ref, w_ref, h_ref, o_ref):
    o_ref[...] = h_ref[...] + jnp.dot(y_ref[...], w_ref[...], preferred_element_type=F32)


def _out_proj(y, w_o, h, layer):
    rows, d = h.shape
    tm = _pick(rows, (1056, 768, 384, 128))
    tn = _pick(d, (512, 256, 128))
    return pl.pallas_call(
        _out_kernel,
        grid=(rows // tm, d // tn),
        in_specs=[
            _single((tm, y.shape[1]), lambda i, n: (i, 0)),
            pl.BlockSpec((None, y.shape[1], tn), lambda i, n: (layer, 0, n)),
            pl.BlockSpec((tm, tn), lambda i, n: (i, n)),
        ],
        out_specs=pl.BlockSpec((tm, tn), lambda i, n: (i, n)),
        out_shape=jax.ShapeDtypeStruct((rows, d), F32),
        compiler_params=_params(("arbitrary", "arbitrary")),
        name="out_proj",
    )(y, w_o, h)


def _final_norm_kernel(h_ref, g_ref, o_ref):
    x = h_ref[...]
    ms = jnp.mean(x * x, axis=-1, keepdims=True)
    o_ref[...] = x * lax.rsqrt(ms + EPS) * g_ref[...]


def _final_norm(h, gain):
    rows, d = h.shape
    tm = _pick(rows, (1056, 768, 384, 128))
    return pl.pallas_call(
        _final_norm_kernel,
        grid=(rows // tm,),
        in_specs=[pl.BlockSpec((tm, d), lambda i: (i, 0)), pl.BlockSpec((1, d), lambda i: (0, 0))],
        out_specs=pl.BlockSpec((tm, d), lambda i: (i, 0)),
        out_shape=jax.ShapeDtypeStruct((rows, d), F32),
        compiler_params=_params(("arbitrary",)),
        name="final_norm",
    )(h, gain)


def kernel(x, meta_tokens, norm_ffn1, ffn1_w_gate, ffn1_w_up, ffn1_w_down, norm_mix, w_in, b_gate, lambda_q1, lambda_k1, lambda_q2, lambda_k2, attn_subln, conv_w, conv_b, gate_x_w, gate_x_b, gate_a_w, gate_a_b, lru_a_param, w_branch_attn, w_branch_lru, w_out, norm_ffn2, ffn2_w_gate, ffn2_w_up, ffn2_w_down, final_norm):
    batch, seq, d = x.shape
    depth = w_in.shape[0]
    attn_v = w_branch_attn.shape[1]
    attn_qk = attn_v
    lru_w = w_branch_lru.shape[1]
    seq_real = N_META + seq
    tp = -(-seq_real // SEQ_BLOCK) * SEQ_BLOCK
    rows = batch * tp

    meta = jnp.broadcast_to(meta_tokens.astype(x.dtype)[None], (batch, N_META, d))
    h = jnp.concatenate([meta, x, jnp.zeros((batch, tp - seq_real, d), x.dtype)], axis=1)
    h = h.reshape(rows, d)

    def vec(p):
        return p[:, None, :]

    bf = lambda w: w.astype(BF16)
    f1g, f1u, f1d = bf(ffn1_w_gate), bf(ffn1_w_up), bf(ffn1_w_down)
    f2g, f2u, f2d = bf(ffn2_w_gate), bf(ffn2_w_up), bf(ffn2_w_down)
    w_in_b, w_ba, w_bl, w_o = bf(w_in), bf(w_branch_attn), bf(w_branch_lru), bf(w_out)
    w_lru_gate = _lru_gate_weights(gate_a_w, gate_x_w)
    lru_prm = jnp.concatenate([conv_b[:, None], gate_a_b[:, None], gate_x_b[:, None],
                               lru_a_param[:, None], conv_w], axis=1)
    rope = _rope_tables(tp)
    n1, nm, n2 = vec(norm_ffn1), vec(norm_mix), vec(norm_ffn2)
    bg = vec(b_gate)
    lq1, lk1, lq2, lk2 = vec(lambda_q1), vec(lambda_k1), vec(lambda_q2), vec(lambda_k2)
    subln = vec(attn_subln)

    c_v = 2 * attn_qk
    c_lru = c_v + attn_v
    c_gate = c_lru + 2 * lru_w
    for l in range(depth):
        lam_init = 0.8 - 0.6 * math.exp(-0.3 * l)
        h = _ffn(h, n1, f1g, f1u, f1d, l)
        xn = _norm(h, nm, l)
        qk = _proj(xn, w_in_b, l, 0, 2 * attn_qk, BF16, "rope", rope, seq_len=tp)
        v = _proj(xn, w_in_b, l, c_v, attn_v, BF16, "plain")
        z_lru = _proj(xn, w_in_b, l, c_lru, 2 * lru_w, F32, "plain")
        gates = _proj(xn, w_in_b, l, c_gate, 2 * d, F32, "gate", bg)
        o_attn = _attention(qk, v, lq1, lk1, lq2, lk2, subln, l, lam_init, batch, tp)
        o_lru = _lru(z_lru, lru_prm, w_lru_gate, l, batch, tp)
        y = _mix(o_attn, o_lru, w_ba, w_bl, gates, l)
        h = _out_proj(y, w_o, h, l)
        h = _ffn(h, n2, f2g, f2u, f2d, l)
    out = _final_norm(h, final_norm[None, :])
    return out.reshape(batch, tp, d)[:, N_META:seq_real]
```

```python
import functools
import math

import jax
import jax.numpy as jnp
from jax import lax
from jax.experimental import pallas as pl
from jax.experimental.pallas import tpu as pltpu

N_META = 16
HEAD_DIM = 128
V_DIM = 2 * HEAD_DIM
ROT_DIM = HEAD_DIM // 4
ROPE_THETA = 500000.0
CONV_W = 4
LRU_C = 8.0
EPS = 1e-6

LANES = 128
SUBLANES = 8
SEQ_BLOCK = 384
ATTN_CHUNK = 2 * SEQ_BLOCK
ATTN_ROWS_PER_DOT = SEQ_BLOCK
LRU_SUPER = 640
VMEM_LIMIT_BYTES = 56 * 1024 * 1024
NEG_BIG = -1e30

BF16 = jnp.bfloat16
F32 = jnp.float32


def _pick(n, prefs):
    for p in prefs:
        if p <= n and n % p == 0:
            return p
    return n


def _params(semantics):
    return pltpu.CompilerParams(dimension_semantics=semantics, vmem_limit_bytes=VMEM_LIMIT_BYTES)


def _sigmoid(x):
    return 0.5 * jnp.tanh(0.5 * x) + 0.5


def _rms_norm_bf16(x, gain):
    ms = jnp.mean(x * x, axis=-1, keepdims=True)
    return (x * lax.rsqrt(ms + EPS) * gain).astype(BF16)


def _ffn_kernel(h_ref, gain_ref, wg_ref, wu_ref, wd_ref, o_ref, xn_ref):
    f = pl.program_id(1)

    @pl.when(f == 0)
    def _():
        x = h_ref[...]
        xn_ref[...] = _rms_norm_bf16(x, gain_ref[...])
        o_ref[...] = x

    xn = xn_ref[...]
    g = jnp.dot(xn, wg_ref[...], preferred_element_type=F32)
    u = jnp.dot(xn, wu_ref[...], preferred_element_type=F32)
    a = (0.5 * (g * _sigmoid(g)) * u).astype(BF16)
    o_ref[...] += jnp.dot(a, wd_ref[...], preferred_element_type=F32)


def _ffn(h, gain, wg, wu, wd, layer):
    rows, d = h.shape
    ff = wg.shape[-1]
    tm = _pick(rows, (1056, 768, 512, 384, 256, 128))
    tf = _pick(ff, (512, 256, 128))
    return pl.pallas_call(
        _ffn_kernel,
        grid=(rows // tm, ff // tf),
        in_specs=[
            pl.BlockSpec((tm, d), lambda i, f: (i, 0)),
            pl.BlockSpec((None, 1, d), lambda i, f: (layer, 0, 0)),
            pl.BlockSpec((None, d, tf), lambda i, f: (layer, 0, f)),
            pl.BlockSpec((None, d, tf), lambda i, f: (layer, 0, f)),
            pl.BlockSpec((None, tf, d), lambda i, f: (layer, f, 0)),
        ],
        out_specs=pl.BlockSpec((tm, d), lambda i, f: (i, 0)),
        out_shape=jax.ShapeDtypeStruct((rows, d), F32),
        scratch_shapes=[pltpu.VMEM((tm, d), BF16)],
        compiler_params=_params(("arbitrary", "arbitrary")),
        name="ffn",
    )(h, gain, wg, wu, wd)


def _norm_kernel(h_ref, gain_ref, o_ref):
    o_ref[...] = _rms_norm_bf16(h_ref[...], gain_ref[...])


def _norm(h, gain, layer):
    rows, d = h.shape
    tm = _pick(rows, (1056, 768, 384, 128))
    return pl.pallas_call(
        _norm_kernel,
        grid=(rows // tm,),
        in_specs=[pl.BlockSpec((tm, d), lambda i: (i, 0)),
                  pl.BlockSpec((None, 1, d), lambda i: (layer, 0, 0))],
        out_specs=pl.BlockSpec((tm, d), lambda i: (i, 0)),
        out_shape=jax.ShapeDtypeStruct((rows, d), BF16),
        compiler_params=_params(("arbitrary",)),
        name="mix_norm",
    )(h, gain)


def _proj_kernel(kind, xn_ref, w_ref, *rest):
    o_ref = rest[-1]
    tm, tn = o_ref.shape
    chunks = {"rope": 4, "gate": 2, "plain": 1}[kind]
    cm = tm // chunks
    for r in range(chunks):
        rows = slice(r * cm, (r + 1) * cm)
        z = jnp.dot(xn_ref[rows, :], w_ref[...], preferred_element_type=F32)
        if kind == "rope":
            tab_ref = rest[0]
            reps = tn // LANES
            cos = jnp.concatenate([tab_ref[0, rows, :]] * reps, axis=1)
            sin_lo = jnp.concatenate([tab_ref[1, rows, :]] * reps, axis=1)
            sin_hi = jnp.concatenate([tab_ref[2, rows, :]] * reps, axis=1)
            half = ROT_DIM // 2
            z = z * cos + pltpu.roll(z, half, 1) * sin_lo + pltpu.roll(z, tn - half, 1) * sin_hi
        elif kind == "gate":
            z = _sigmoid(z + rest[0][...])
        o_ref[rows, :] = z.astype(o_ref.dtype)


def _proj(xn, w, layer, col0, ncols, out_dtype, kind, extra=None, seq_len=None):
    rows, d = xn.shape
    tm = _pick(seq_len, (2112, 1408, 768, 384, 128)) if seq_len else _pick(rows, (2112, 1408, 768, 384, 128))
    tn = _pick(ncols // 2 if kind == "rope" else ncols, (512, 256, 128))
    assert col0 % tn == 0
    cb0 = col0 // tn
    in_specs = [
        pl.BlockSpec((tm, d), lambda i, n: (i, 0)),
        pl.BlockSpec((None, d, tn), lambda i, n: (layer, 0, cb0 + n)),
    ]
    args = [xn, w]
    if kind == "rope":
        tiles_per_seq = seq_len // tm
        q_blocks = ncols // 2 // tn
        in_specs.append(pl.BlockSpec((None, 3, tm, LANES),
                                     lambda i, n: (n // q_blocks, 0, i % tiles_per_seq, 0)))
        args.append(extra)
    elif kind == "gate":
        in_specs.append(pl.BlockSpec((None, 1, tn), lambda i, n: (layer, 0, n)))
        args.append(extra)
    return pl.pallas_call(
        functools.partial(_proj_kernel, kind),
        grid=(rows // tm, ncols // tn),
        in_specs=in_specs,
        out_specs=pl.BlockSpec((tm, tn), lambda i, n: (i, n)),
        out_shape=jax.ShapeDtypeStruct((rows, ncols), out_dtype),
        compiler_params=_params(("arbitrary", "arbitrary")),
        name="proj_" + kind,
    )(*args)


def _rope_tables(seq_len):
    half = ROT_DIM // 2
    inv_freq = ROPE_THETA ** (-jnp.arange(0, ROT_DIM, 2, dtype=F32) / ROT_DIM)
    ang = jnp.arange(seq_len, dtype=F32)[:, None] * inv_freq[None, :]
    cos, sin = jnp.cos(ang), jnp.sin(ang)
    ones = jnp.ones((seq_len, LANES - ROT_DIM), F32)
    zeros = jnp.zeros((seq_len, LANES - ROT_DIM), F32)
    zh = jnp.zeros((seq_len, half), F32)
    c = jnp.concatenate([cos, cos, ones], axis=1)
    s_lo = jnp.concatenate([zh, sin, zeros], axis=1)
    s_hi = jnp.concatenate([-sin, zh, zeros], axis=1)
    k_tab = jnp.stack([c, s_lo, s_hi])
    return jnp.stack([k_tab * ((HEAD_DIM ** -0.5) * math.log2(math.e)), k_tab])


def _attn_stages(k_ref, v_ref, qs, m_ref, l_ref, acc_ref, p_refs, a_refs):
    tq = qs[0].shape[0]
    rows_per_dot = min(tq, ATTN_ROWS_PER_DOT)

    def score_stage(start, width, masked, slot):
        kb = k_ref[pl.ds(start, width), :]
        for c in range(2):
            kc = kb[:, c * HEAD_DIM:(c + 1) * HEAD_DIM]
            for r0 in range(0, tq, rows_per_dot):
                rows = slice(r0, r0 + rows_per_dot)
                s = lax.dot_general(qs[c][rows], kc, (((1,), (1,)), ((), ())), preferred_element_type=F32)
                if masked:
                    row = lax.broadcasted_iota(jnp.int32, s.shape, 0) + r0
                    col = lax.broadcasted_iota(jnp.int32, s.shape, 1)
                    s = jnp.where(col <= row, s, NEG_BIG)
                m_old = m_ref[c, rows, :]
                m_new = jnp.maximum(m_old, jnp.max(s, axis=-1, keepdims=True))
                alpha = jnp.exp2(m_old - m_new)
                p = jnp.exp2(s - jnp.tile(m_new, (1, width // LANES)))
                l_ref[c, rows, :] = alpha * l_ref[c, rows, :] + jnp.sum(p, axis=-1, keepdims=True)
                m_ref[c, rows, :] = m_new
                a_refs[slot][c, rows, :] = alpha
                p_refs[slot][c, rows, 0:width] = p.astype(BF16)

    def value_stage(start, width, slot):
        vb = v_ref[pl.ds(start, width), :]
        for c in range(2):
            for r0 in range(0, tq, rows_per_dot):
                rows = slice(r0, r0 + rows_per_dot)
                pv = jnp.dot(p_refs[slot][c, rows, 0:width], vb, preferred_element_type=F32)
                acc_ref[c, rows, :] = (jnp.tile(a_refs[slot][c, rows, :], (1, V_DIM // LANES))
                                       * acc_ref[c, rows, :] + pv)

    return score_stage, value_stage


def _attn_init(m_ref, l_ref, acc_ref):
    m_ref[...] = jnp.full(m_ref.shape, NEG_BIG, F32)
    l_ref[...] = jnp.zeros(l_ref.shape, F32)
    acc_ref[...] = jnp.zeros(acc_ref.shape, F32)


def _attn_finish(lam_init, lq1_ref, lk1_ref, lq2_ref, lk2_ref, subln_ref, o_ref, l_ref, acc_ref):
    lam = (jnp.exp(jnp.sum(lq1_ref[...] * lk1_ref[...], keepdims=True))
           - jnp.exp(jnp.sum(lq2_ref[...] * lk2_ref[...], keepdims=True)) + lam_init)
    reps = V_DIM // LANES
    o = (acc_ref[0] * jnp.tile(1.0 / l_ref[0], (1, reps))
         - lam * (acc_ref[1] * jnp.tile(1.0 / l_ref[1], (1, reps))))
    o = o * lax.rsqrt(jnp.mean(o * o, axis=-1, keepdims=True) + EPS)
    o = o * subln_ref[...] * (1.0 - lam_init)
    o_ref[...] = o.astype(o_ref.dtype)


def _attn_main_kernel(lam_init, q_ref, k_ref, v_ref, lq1_ref, lk1_ref, lq2_ref, lk2_ref, subln_ref,
                      o_ref, m_ref, l_ref, acc_ref, p0_ref, p1_ref, a0_ref, a1_ref):
    i = pl.program_id(2)
    tq = q_ref.shape[0]
    q = q_ref[...]
    score_stage, value_stage = _attn_stages(k_ref, v_ref, (q[:, :HEAD_DIM], q[:, HEAD_DIM:]),
                                            m_ref, l_ref, acc_ref, (p0_ref, p1_ref), (a0_ref, a1_ref))
    _attn_init(m_ref, l_ref, acc_ref)
    score_stage(pl.multiple_of(i * tq, tq), tq, True, 0)

    def chunk_start(n):
        return pl.multiple_of(jnp.where(n == 0, i, n - 1) * tq, tq)

    def body(n, carry):
        for slot in range(2):
            @pl.when(n % 2 == slot)
            def _():
                score_stage(chunk_start(n), tq, False, slot)
                value_stage(chunk_start(n - 1), tq, 1 - slot)
        return carry

    lax.fori_loop(1, i + 1, body, 0)
    for slot in range(2):
        @pl.when(i % 2 == slot)
        def _():
            value_stage(chunk_start(i), tq, slot)
    _attn_finish(lam_init, lq1_ref, lk1_ref, lq2_ref, lk2_ref, subln_ref, o_ref, l_ref, acc_ref)


def _attn_tail_kernel(lam_init, tk, q_ref, k_ref, v_ref, lq1_ref, lk1_ref, lq2_ref, lk2_ref, subln_ref,
                      o_in_ref, o_ref, m_ref, l_ref, acc_ref, p0_ref, p1_ref, a0_ref, a1_ref):
    del o_in_ref
    tq = q_ref.shape[0]
    n_full = (k_ref.shape[0] - tq) // tk
    q = q_ref[...]
    score_stage, value_stage = _attn_stages(k_ref, v_ref, (q[:, :HEAD_DIM], q[:, HEAD_DIM:]),
                                            m_ref, l_ref, acc_ref, (p0_ref, p1_ref), (a0_ref, a1_ref))
    _attn_init(m_ref, l_ref, acc_ref)
    chunks = [(n_full * tk, tq)] + [(j * tk, tk) for j in range(n_full)]
    score_stage(chunks[0][0], chunks[0][1], True, 0)
    for n in range(1, len(chunks)):
        score_stage(chunks[n][0], chunks[n][1], False, n % 2)
        value_stage(chunks[n - 1][0], chunks[n - 1][1], (n - 1) % 2)
    value_stage(chunks[-1][0], chunks[-1][1], (len(chunks) - 1) % 2)
    _attn_finish(lam_init, lq1_ref, lk1_ref, lq2_ref, lk2_ref, subln_ref, o_ref, l_ref, acc_ref)


def _attn_scratch(tq, tk):
    return [
        pltpu.VMEM((2, tq, LANES), F32),
        pltpu.VMEM((2, tq, LANES), F32),
        pltpu.VMEM((2, tq, V_DIM), F32),
        pltpu.VMEM((2, tq, tk), BF16),
        pltpu.VMEM((2, tq, tk), BF16),
        pltpu.VMEM((2, tq, LANES), F32),
        pltpu.VMEM((2, tq, LANES), F32),
    ]


def _attention(qk, v, lq1, lk1, lq2, lk2, subln, layer, lam_init, batch, seq_len):
    n_heads = v.shape[1] // V_DIM
    tk = ATTN_CHUNK
    nq = seq_len // tk
    rem = seq_len - nq * tk
    qk3 = qk.reshape(batch, seq_len, qk.shape[1])
    v3 = v.reshape(batch, seq_len, v.shape[1])
    out_shape = jax.ShapeDtypeStruct((batch, seq_len, n_heads * V_DIM), BF16)

    def kv_param_specs(grid_rank):
        def at(*idx):
            return (lambda b, h, i: idx_of(b, h, idx)) if grid_rank == 3 else (lambda b, h: idx_of(b, h, idx))

        def idx_of(b, h, idx):
            return tuple(f(b, h) if callable(f) else f for f in idx)

        lam_spec = pl.BlockSpec((None, 1, HEAD_DIM), at(layer, 0, 0))
        return [pl.BlockSpec((None, seq_len, V_DIM), at(lambda b, h: b, 0, lambda b, h: n_heads + h)),
                pl.BlockSpec((None, seq_len, V_DIM), at(lambda b, h: b, 0, lambda b, h: h)),
                lam_spec, lam_spec, lam_spec, lam_spec,
                pl.BlockSpec((None, 1, V_DIM), at(layer, 0, 0))]

    out = pl.pallas_call(
        functools.partial(_attn_main_kernel, lam_init),
        grid=(batch, n_heads, nq),
        in_specs=[pl.BlockSpec((None, tk, V_DIM), lambda b, h, i: (b, i, h))] + kv_param_specs(3),
        out_specs=pl.BlockSpec((None, tk, V_DIM), lambda b, h, i: (b, i, h)),
        out_shape=out_shape,
        scratch_shapes=_attn_scratch(tk, tk),
        compiler_params=_params(("arbitrary", "arbitrary", "arbitrary")),
        name="attn_main",
    )(qk3, qk3, v3, lq1, lk1, lq2, lk2, subln)
    if rem:
        assert seq_len % rem == 0
        last = seq_len // rem - 1
        out = pl.pallas_call(
            functools.partial(_attn_tail_kernel, lam_init, tk),
            grid=(batch, n_heads),
            in_specs=([pl.BlockSpec((None, rem, V_DIM), lambda b, h: (b, last, h))] + kv_param_specs(2)
                      + [pl.BlockSpec(memory_space=pl.ANY)]),
            out_specs=pl.BlockSpec((None, rem, V_DIM), lambda b, h: (b, last, h)),
            out_shape=out_shape,
            scratch_shapes=_attn_scratch(rem, tk),
            input_output_aliases={8: 0},
            compiler_params=_params(("arbitrary", "arbitrary")),
            name="attn_tail",
        )(qk3, qk3, v3, lq1, lk1, lq2, lk2, subln, out)
    return out.reshape(batch * seq_len, n_heads * V_DIM)


def _lru_kernel(xr_ref, yr_ref, p_ref, w_ref, o_ref, halo_ref, carry_ref, a_ref, b_ref, h_ref,
                hz_ref, ac_ref):
    t = pl.program_id(2)
    tt, cw = xr_ref.shape
    halo = SUBLANES

    @pl.when(t == 0)
    def _():
        halo_ref[...] = jnp.zeros(halo_ref.shape, F32)
        carry_ref[...] = jnp.zeros(carry_ref.shape, F32)

    x = xr_ref[...]
    prev = halo_ref[...]
    prm = p_ref[...]
    conv_b, ga_b, gx_b, a_par = prm[0:1], prm[1:2], prm[2:3], prm[3:4]
    row8 = lax.broadcasted_iota(jnp.int32, (halo, cw), 0)
    xc = conv_b + x * prm[4:5]
    for j in range(1, CONV_W):
        xs = pltpu.roll(x, j, 0)
        head = jnp.where(row8 < j, pltpu.roll(prev, j, 0), xs[0:halo, :])
        xc = xc + jnp.concatenate([head, xs[halo:, :]], axis=0) * prm[4 + j:5 + j]
    halo_ref[...] = x[tt - halo:tt, :]

    gz = jnp.dot(xc.astype(BF16), w_ref[...], preferred_element_type=F32)
    r = _sigmoid(gz[:, :cw] + ga_b)
    ig = _sigmoid(gz[:, cw:] + gx_b)
    log_a = LRU_C * r * jax.nn.log_sigmoid(a_par)
    a = jnp.exp(log_a)
    y = -jnp.tanh(log_a) * (a * a + 1.0)
    mult = jnp.where(y > 0.0, y * lax.rsqrt(y), 0.0)
    bx = mult * (ig * xc)
    nl = cw // LANES
    for k in range(nl):
        a_ref[k] = a[:, k * LANES:(k + 1) * LANES]
        b_ref[k] = bx[:, k * LANES:(k + 1) * LANES]

    ng = tt // SUBLANES

    def strided(ref, g):
        return jnp.concatenate([ref[k, pl.ds(g, SUBLANES, stride=ng), :] for k in range(nl)], axis=1)

    h = jnp.zeros((SUBLANES, cw), F32)
    cum = jnp.ones((SUBLANES, cw), F32)
    for g in range(ng):
        ag = strided(a_ref, g)
        h = ag * h + strided(b_ref, g)
        cum = ag * cum
        hz_ref[g] = h
        ac_ref[g] = cum
    e = carry_ref[...]
    entry = []
    for s in range(SUBLANES):
        entry.append(e)
        e = cum[s:s + 1, :] * e + h[s:s + 1, :]
    carry_ref[...] = e
    entry = jnp.concatenate(entry, axis=0)
    for g in range(ng):
        hg = hz_ref[g] + ac_ref[g] * entry
        for k in range(nl):
            h_ref[k, pl.ds(g, SUBLANES, stride=ng), :] = hg[:, k * LANES:(k + 1) * LANES]

    hs = jnp.concatenate([h_ref[k] for k in range(nl)], axis=1)
    o_ref[...] = (hs * jax.nn.gelu(yr_ref[...])).astype(o_ref.dtype)


def _lru(z_lru, prm, w_gate, layer, batch, seq_len):
    rows = z_lru.shape[0]
    width = z_lru.shape[1] // 2
    cw = LRU_SUPER
    ns = width // cw
    tt = SEQ_BLOCK
    nt = seq_len // tt
    return pl.pallas_call(
        _lru_kernel,
        grid=(batch, ns, nt),
        in_specs=[
            pl.BlockSpec((tt, cw), lambda b, s, t: (b * nt + t, s)),
            pl.BlockSpec((tt, cw), lambda b, s, t: (b * nt + t, ns + s)),
            pl.BlockSpec((None, 2 * CONV_W, cw), lambda b, s, t: (layer, 0, s)),
            pl.BlockSpec((None, None, cw, 2 * cw), lambda b, s, t: (layer, s, 0, 0)),
        ],
        out_specs=pl.BlockSpec((tt, cw), lambda b, s, t: (b * nt + t, s)),
        out_shape=jax.ShapeDtypeStruct((rows, width), BF16),
        scratch_shapes=[
            pltpu.VMEM((SUBLANES, cw), F32),
            pltpu.VMEM((1, cw), F32),
            pltpu.VMEM((cw // LANES, tt, LANES), F32),
            pltpu.VMEM((cw // LANES, tt, LANES), F32),
            pltpu.VMEM((cw // LANES, tt, LANES), F32),
            pltpu.VMEM((tt // SUBLANES, SUBLANES, cw), F32),
            pltpu.VMEM((tt // SUBLANES, SUBLANES, cw), F32),
        ],
        compiler_params=_params(("arbitrary", "arbitrary", "arbitrary")),
        name="rglru",
    )(z_lru, z_lru, prm, w_gate)


def _lru_gate_weights(ga_w, gx_w):
    depth, nb, bw, _ = ga_w.shape
    per = LRU_SUPER // bw
    ns = nb // per

    def dense(w):
        w = w.reshape(depth, ns, per, bw, bw)
        eye = jnp.eye(per, dtype=w.dtype)
        full = jnp.einsum("lspkj,pq->lspkqj", w, eye)
        return full.reshape(depth, ns, per * bw, per * bw)

    return jnp.concatenate([dense(ga_w), dense(gx_w)], axis=-1).astype(BF16)


def _mix_kernel(oa_ref, ol_ref, wa_ref, wl_ref, ga_ref, gl_ref, o_ref):
    ya = jnp.dot(oa_ref[...], wa_ref[...], preferred_element_type=F32)
    yl = jnp.dot(ol_ref[...], wl_ref[...], preferred_element_type=F32)
    o_ref[...] = (ga_ref[...] * ya + gl_ref[...] * yl).astype(o_ref.dtype)


def _mix(oa, ol, w_ba, w_bl, gates, layer):
    rows, da = oa.shape
    dl = ol.shape[1]
    d = w_ba.shape[-1]
    tm = _pick(rows, (1056, 768, 384, 128))
    tn = _pick(d, (512, 256, 128))
    nb = d // tn
    return pl.pallas_call(
        _mix_kernel,
        grid=(rows // tm, nb),
        in_specs=[
            pl.BlockSpec((tm, da), lambda i, n: (i, 0)),
            pl.BlockSpec((tm, dl), lambda i, n: (i, 0)),
            pl.BlockSpec((None, da, tn), lambda i, n: (layer, 0, n)),
            pl.BlockSpec((None, dl, tn), lambda i, n: (layer, 0, n)),
            pl.BlockSpec((tm, tn), lambda i, n: (i, n)),
            pl.BlockSpec((tm, tn), lambda i, n: (i, nb + n)),
        ],
        out_specs=pl.BlockSpec((tm, tn), lambda i, n: (i, n)),
        out_shape=jax.ShapeDtypeStruct((rows, d), BF16),
        compiler_params=_params(("arbitrary", "arbitrary")),
        name="branch_mix",
    )(oa, ol, w_ba, w_bl, gates, gates)


def _out_kernel(y_ref, w_ref, h_ref, o_ref):
    o_ref[...] = h_ref[...] + jnp.dot(y_ref[...], w_ref[...], preferred_element_type=F32)


def _out_proj(y, w_o, h, layer):
    rows, d = h.shape
    tm = _pick(rows, (1056, 768, 384, 128))
    tn = _pick(d, (512, 256, 128))
    return pl.pallas_call(
        _out_kernel,
        grid=(rows // tm, d // tn),
        in_specs=[
            pl.BlockSpec((tm, y.shape[1]), lambda i, n: (i, 0)),
            pl.BlockSpec((None, y.shape[1], tn), lambda i, n: (layer, 0, n)),
            pl.BlockSpec((tm, tn), lambda i, n: (i, n)),
        ],
        out_specs=pl.BlockSpec((tm, tn), lambda i, n: (i, n)),
        out_shape=jax.ShapeDtypeStruct((rows, d), F32),
        compiler_params=_params(("arbitrary", "arbitrary")),
        name="out_proj",
    )(y, w_o, h)


def _final_norm_kernel(h_ref, next_ref, g_ref, o_ref):
    x = jnp.concatenate([h_ref[N_META:, :], next_ref[...]], axis=0)
    ms = jnp.mean(x * x, axis=-1, keepdims=True)
    o_ref[...] = x * lax.rsqrt(ms + EPS) * g_ref[...]


def _final_norm(h, gain, batch, seq_len, seq):
    d = h.shape[1]
    tb = _pick(seq, (512, 256, 128))
    assert tb % N_META == 0 and seq + N_META <= seq_len
    h3 = h.reshape(batch, seq_len, d)
    return pl.pallas_call(
        _final_norm_kernel,
        grid=(batch, seq // tb),
        in_specs=[pl.BlockSpec((None, tb, d), lambda b, j: (b, j, 0)),
                  pl.BlockSpec((None, N_META, d), lambda b, j: (b, (j + 1) * (tb // N_META), 0)),
                  pl.BlockSpec((1, d), lambda b, j: (0, 0))],
        out_specs=pl.BlockSpec((None, tb, d), lambda b, j: (b, j, 0)),
        out_shape=jax.ShapeDtypeStruct((batch, seq, d), F32),
        compiler_params=_params(("arbitrary", "arbitrary")),
        name="final_norm",
    )(h3, h3, gain)


def kernel(x, meta_tokens, norm_ffn1, ffn1_w_gate, ffn1_w_up, ffn1_w_down, norm_mix, w_in, b_gate, lambda_q1, lambda_k1, lambda_q2, lambda_k2, attn_subln, conv_w, conv_b, gate_x_w, gate_x_b, gate_a_w, gate_a_b, lru_a_param, w_branch_attn, w_branch_lru, w_out, norm_ffn2, ffn2_w_gate, ffn2_w_up, ffn2_w_down, final_norm):
    batch, seq, d = x.shape
    depth = w_in.shape[0]
    attn_v = w_branch_attn.shape[1]
    attn_qk = attn_v
    lru_w = w_branch_lru.shape[1]
    seq_real = N_META + seq
    tp = -(-seq_real // SEQ_BLOCK) * SEQ_BLOCK
    rows = batch * tp

    meta = jnp.broadcast_to(meta_tokens.astype(x.dtype)[None], (batch, N_META, d))
    h = jnp.concatenate([meta, x, jnp.zeros((batch, tp - seq_real, d), x.dtype)], axis=1)
    h = h.reshape(rows, d)

    def vec(p):
        return p[:, None, :]

    bf = lambda w: w.astype(BF16)
    f1g, f1u, f1d = bf(ffn1_w_gate), bf(ffn1_w_up), bf(ffn1_w_down)
    f2g, f2u, f2d = bf(ffn2_w_gate), bf(ffn2_w_up), bf(ffn2_w_down)
    w_in_b, w_ba, w_bl, w_o = bf(w_in), bf(w_branch_attn), bf(w_branch_lru), bf(w_out)
    w_lru_gate = _lru_gate_weights(gate_a_w, gate_x_w)
    lru_prm = jnp.concatenate([conv_b[:, None], gate_a_b[:, None], gate_x_b[:, None],
                               lru_a_param[:, None], conv_w], axis=1)
    rope = _rope_tables(tp)
    n1, nm, n2 = vec(norm_ffn1), vec(norm_mix), vec(norm_ffn2)
    bg = vec(b_gate)
    lq1, lk1, lq2, lk2 = vec(lambda_q1), vec(lambda_k1), vec(lambda_q2), vec(lambda_k2)
    subln = vec(attn_subln)

    c_v = 2 * attn_qk
    c_lru = c_v + attn_v
    c_gate = c_lru + 2 * lru_w
    for l in range(depth):
        lam_init = 0.8 - 0.6 * math.exp(-0.3 * l)
        h = _ffn(h, n1, f1g, f1u, f1d, l)
        xn = _norm(h, nm, l)
        qk = _proj(xn, w_in_b, l, 0, 2 * attn_qk, BF16, "rope", rope, seq_len=tp)
        v = _proj(xn, w_in_b, l, c_v, attn_v, BF16, "plain")
        z_lru = _proj(xn, w_in_b, l, c_lru, 2 * lru_w, F32, "plain")
        gates = _proj(xn, w_in_b, l, c_gate, 2 * d, F32, "gate", bg)
        o_attn = _attention(qk, v, lq1, lk1, lq2, lk2, subln, l, lam_init, batch, tp)
        o_lru = _lru(z_lru, lru_prm, w_lru_gate, l, batch, tp)
        y = _mix(o_attn, o_lru, w_ba, w_bl, gates, l)
        h = _out_proj(y, w_o, h, l)
        h = _ffn(h, n2, f2g, f2u, f2d, l)
    return _final_norm(h, final_norm[None, :], batch, tp, seq)
```

```python
import functools
import math

import jax
import jax.numpy as jnp
from jax import lax
from jax.experimental import pallas as pl
from jax.experimental.pallas import tpu as pltpu

N_META = 16
HEAD_DIM = 128
V_DIM = 2 * HEAD_DIM
ROT_DIM = HEAD_DIM // 4
ROPE_THETA = 500000.0
CONV_W = 4
LRU_C = 8.0
EPS = 1e-6

LANES = 128
SUBLANES = 8
SEQ_BLOCK = 384
ATTN_CHUNK = 2 * SEQ_BLOCK
ATTN_ROWS_PER_DOT = SEQ_BLOCK
LRU_SUPER = 640
LRU_ROW_CHUNKS = 2
LRU_SCAN_UNROLL = 8
VMEM_LIMIT_BYTES = 56 * 1024 * 1024
NEG_BIG = -1e30

BF16 = jnp.bfloat16
F32 = jnp.float32


def _pick(n, prefs):
    for p in prefs:
        if p <= n and n % p == 0:
            return p
    return n


def _params(semantics):
    return pltpu.CompilerParams(dimension_semantics=semantics, vmem_limit_bytes=VMEM_LIMIT_BYTES)


def _sigmoid(x):
    return 0.5 * jnp.tanh(0.5 * x) + 0.5


def _rms_norm_bf16(x, gain):
    ms = jnp.mean(x * x, axis=-1, keepdims=True)
    return (x * lax.rsqrt(ms + EPS) * gain).astype(BF16)


def _ffn_kernel(h_ref, gain_ref, wg_ref, wu_ref, wd_ref, o_ref, xn_ref):
    f = pl.program_id(1)

    @pl.when(f == 0)
    def _():
        x = h_ref[...]
        xn_ref[...] = _rms_norm_bf16(x, gain_ref[...])
        o_ref[...] = x

    xn = xn_ref[...]
    g = jnp.dot(xn, wg_ref[...], preferred_element_type=F32)
    u = jnp.dot(xn, wu_ref[...], preferred_element_type=F32)
    a = (0.5 * (g * _sigmoid(g)) * u).astype(BF16)
    o_ref[...] += jnp.dot(a, wd_ref[...], preferred_element_type=F32)


def _ffn(h, gain, wg, wu, wd, layer):
    rows, d = h.shape
    ff = wg.shape[-1]
    tm = _pick(rows, (1056, 768, 512, 384, 256, 128))
    tf = _pick(ff, (512, 256, 128))
    return pl.pallas_call(
        _ffn_kernel,
        grid=(rows // tm, ff // tf),
        in_specs=[
            pl.BlockSpec((tm, d), lambda i, f: (i, 0)),
            pl.BlockSpec((None, 1, d), lambda i, f: (layer, 0, 0)),
            pl.BlockSpec((None, d, tf), lambda i, f: (layer, 0, f)),
            pl.BlockSpec((None, d, tf), lambda i, f: (layer, 0, f)),
            pl.BlockSpec((None, tf, d), lambda i, f: (layer, f, 0)),
        ],
        out_specs=pl.BlockSpec((tm, d), lambda i, f: (i, 0)),
        out_shape=jax.ShapeDtypeStruct((rows, d), F32),
        scratch_shapes=[pltpu.VMEM((tm, d), BF16)],
        compiler_params=_params(("arbitrary", "arbitrary")),
        name="ffn",
    )(h, gain, wg, wu, wd)


def _norm_kernel(h_ref, gain_ref, o_ref):
    o_ref[...] = _rms_norm_bf16(h_ref[...], gain_ref[...])


def _norm(h, gain, layer):
    rows, d = h.shape
    tm = _pick(rows, (1056, 768, 384, 128))
    return pl.pallas_call(
        _norm_kernel,
        grid=(rows // tm,),
        in_specs=[pl.BlockSpec((tm, d), lambda i: (i, 0)),
                  pl.BlockSpec((None, 1, d), lambda i: (layer, 0, 0))],
        out_specs=pl.BlockSpec((tm, d), lambda i: (i, 0)),
        out_shape=jax.ShapeDtypeStruct((rows, d), BF16),
        compiler_params=_params(("arbitrary",)),
        name="mix_norm",
    )(h, gain)


def _proj_kernel(kind, xn_ref, w_ref, *rest):
    o_ref = rest[-1]
    tm, tn = o_ref.shape
    chunks = {"rope": 4, "gate": 2, "plain": 1}[kind]
    cm = tm // chunks
    for r in range(chunks):
        rows = slice(r * cm, (r + 1) * cm)
        z = jnp.dot(xn_ref[rows, :], w_ref[...], preferred_element_type=F32)
        if kind == "rope":
            tab_ref = rest[0]
            reps = tn // LANES
            cos = jnp.concatenate([tab_ref[0, rows, :]] * reps, axis=1)
            sin_lo = jnp.concatenate([tab_ref[1, rows, :]] * reps, axis=1)
            sin_hi = jnp.concatenate([tab_ref[2, rows, :]] * reps, axis=1)
            half = ROT_DIM // 2
            z = z * cos + pltpu.roll(z, half, 1) * sin_lo + pltpu.roll(z, tn - half, 1) * sin_hi
        elif kind == "gate":
            z = _sigmoid(z + rest[0][...])
        o_ref[rows, :] = z.astype(o_ref.dtype)


def _proj(xn, w, layer, col0, ncols, out_dtype, kind, extra=None, seq_len=None):
    rows, d = xn.shape
    tm = _pick(seq_len, (2112, 1408, 768, 384, 128)) if seq_len else _pick(rows, (2112, 1408, 768, 384, 128))
    tn = _pick(ncols // 2 if kind == "rope" else ncols, (512, 256, 128))
    assert col0 % tn == 0
    cb0 = col0 // tn
    in_specs = [
        pl.BlockSpec((tm, d), lambda i, n: (i, 0)),
        pl.BlockSpec((None, d, tn), lambda i, n: (layer, 0, cb0 + n)),
    ]
    args = [xn, w]
    if kind == "rope":
        tiles_per_seq = seq_len // tm
        q_blocks = ncols // 2 // tn
        in_specs.append(pl.BlockSpec((None, 3, tm, LANES),
                                     lambda i, n: (n // q_blocks, 0, i % tiles_per_seq, 0)))
        args.append(extra)
    elif kind == "gate":
        in_specs.append(pl.BlockSpec((None, 1, tn), lambda i, n: (layer, 0, n)))
        args.append(extra)
    return pl.pallas_call(
        functools.partial(_proj_kernel, kind),
        grid=(rows // tm, ncols // tn),
        in_specs=in_specs,
        out_specs=pl.BlockSpec((tm, tn), lambda i, n: (i, n)),
        out_shape=jax.ShapeDtypeStruct((rows, ncols), out_dtype),
        compiler_params=_params(("arbitrary", "arbitrary")),
        name="proj_" + kind,
    )(*args)


def _rope_tables(seq_len):
    half = ROT_DIM // 2
    inv_freq = ROPE_THETA ** (-jnp.arange(0, ROT_DIM, 2, dtype=F32) / ROT_DIM)
    ang = jnp.arange(seq_len, dtype=F32)[:, None] * inv_freq[None, :]
    cos, sin = jnp.cos(ang), jnp.sin(ang)
    ones = jnp.ones((seq_len, LANES - ROT_DIM), F32)
    zeros = jnp.zeros((seq_len, LANES - ROT_DIM), F32)
    zh = jnp.zeros((seq_len, half), F32)
    c = jnp.concatenate([cos, cos, ones], axis=1)
    s_lo = jnp.concatenate([zh, sin, zeros], axis=1)
    s_hi = jnp.concatenate([-sin, zh, zeros], axis=1)
    k_tab = jnp.stack([c, s_lo, s_hi])
    return jnp.stack([k_tab * ((HEAD_DIM ** -0.5) * math.log2(math.e)), k_tab])


def _attn_stages(k_ref, v_ref, qs, m_ref, l_ref, acc_ref, p_refs, a_refs):
    tq = qs[0].shape[0]
    rows_per_dot = min(tq, ATTN_ROWS_PER_DOT)

    def score_stage(start, width, masked, slot):
        kb = k_ref[pl.ds(start, width), :]
        for c in range(2):
            kc = kb[:, c * HEAD_DIM:(c + 1) * HEAD_DIM]
            for r0 in range(0, tq, rows_per_dot):
                rows = slice(r0, r0 + rows_per_dot)
                kw = min(width, r0 + rows_per_dot) if masked else width
                s = lax.dot_general(qs[c][rows], kc[0:kw], (((1,), (1,)), ((), ())),
                                    preferred_element_type=F32)
                if masked:
                    row = lax.broadcasted_iota(jnp.int32, s.shape, 0) + r0
                    col = lax.broadcasted_iota(jnp.int32, s.shape, 1)
                    s = jnp.where(col <= row, s, NEG_BIG)
                m_old = m_ref[c, rows, :]
                m_new = jnp.maximum(m_old, jnp.max(s, axis=-1, keepdims=True))
                alpha = jnp.exp2(m_old - m_new)
                p = jnp.exp2(s - jnp.tile(m_new, (1, kw // LANES)))
                l_ref[c, rows, :] = alpha * l_ref[c, rows, :] + jnp.sum(p, axis=-1, keepdims=True)
                m_ref[c, rows, :] = m_new
                a_refs[slot][c, rows, :] = alpha
                p_refs[slot][c, rows, 0:kw] = p.astype(BF16)
                if kw < width:
                    p_refs[slot][c, rows, kw:width] = jnp.zeros((rows_per_dot, width - kw), BF16)

    def value_stage(start, width, slot):
        vb = v_ref[pl.ds(start, width), :]
        for c in range(2):
            for r0 in range(0, tq, rows_per_dot):
                rows = slice(r0, r0 + rows_per_dot)
                pv = jnp.dot(p_refs[slot][c, rows, 0:width], vb, preferred_element_type=F32)
                acc_ref[c, rows, :] = (jnp.tile(a_refs[slot][c, rows, :], (1, V_DIM // LANES))
                                       * acc_ref[c, rows, :] + pv)

    return score_stage, value_stage


def _attn_init(m_ref, l_ref, acc_ref):
    m_ref[...] = jnp.full(m_ref.shape, NEG_BIG, F32)
    l_ref[...] = jnp.zeros(l_ref.shape, F32)
    acc_ref[...] = jnp.zeros(acc_ref.shape, F32)


def _attn_finish(lam_init, lq1_ref, lk1_ref, lq2_ref, lk2_ref, subln_ref, o_ref, l_ref, acc_ref):
    lam = (jnp.exp(jnp.sum(lq1_ref[...] * lk1_ref[...], keepdims=True))
           - jnp.exp(jnp.sum(lq2_ref[...] * lk2_ref[...], keepdims=True)) + lam_init)
    reps = V_DIM // LANES
    o = (acc_ref[0] * jnp.tile(1.0 / l_ref[0], (1, reps))
         - lam * (acc_ref[1] * jnp.tile(1.0 / l_ref[1], (1, reps))))
    o = o * lax.rsqrt(jnp.mean(o * o, axis=-1, keepdims=True) + EPS)
    o = o * subln_ref[...] * (1.0 - lam_init)
    o_ref[...] = o.astype(o_ref.dtype)


def _attn_main_kernel(lam_init, q_ref, k_ref, v_ref, lq1_ref, lk1_ref, lq2_ref, lk2_ref, subln_ref,
                      o_ref, m_ref, l_ref, acc_ref, p0_ref, p1_ref, a0_ref, a1_ref):
    i = pl.program_id(2)
    tq = q_ref.shape[0]
    q = q_ref[...]
    score_stage, value_stage = _attn_stages(k_ref, v_ref, (q[:, :HEAD_DIM], q[:, HEAD_DIM:]),
                                            m_ref, l_ref, acc_ref, (p0_ref, p1_ref), (a0_ref, a1_ref))
    _attn_init(m_ref, l_ref, acc_ref)
    score_stage(pl.multiple_of(i * tq, tq), tq, True, 0)

    def chunk_start(n):
        return pl.multiple_of(jnp.where(n == 0, i, n - 1) * tq, tq)

    def body(n, carry):
        for slot in range(2):
            @pl.when(n % 2 == slot)
            def _():
                score_stage(chunk_start(n), tq, False, slot)
                value_stage(chunk_start(n - 1), tq, 1 - slot)
        return carry

    lax.fori_loop(1, i + 1, body, 0)
    for slot in range(2):
        @pl.when(i % 2 == slot)
        def _():
            value_stage(chunk_start(i), tq, slot)
    _attn_finish(lam_init, lq1_ref, lk1_ref, lq2_ref, lk2_ref, subln_ref, o_ref, l_ref, acc_ref)


def _attn_tail_kernel(lam_init, tk, q_ref, k_ref, v_ref, lq1_ref, lk1_ref, lq2_ref, lk2_ref, subln_ref,
                      o_in_ref, o_ref, m_ref, l_ref, acc_ref, p0_ref, p1_ref, a0_ref, a1_ref):
    del o_in_ref
    tq = q_ref.shape[0]
    n_full = (k_ref.shape[0] - tq) // tk
    q = q_ref[...]
    score_stage, value_stage = _attn_stages(k_ref, v_ref, (q[:, :HEAD_DIM], q[:, HEAD_DIM:]),
                                            m_ref, l_ref, acc_ref, (p0_ref, p1_ref), (a0_ref, a1_ref))
    _attn_init(m_ref, l_ref, acc_ref)
    chunks = [(n_full * tk, tq)] + [(j * tk, tk) for j in range(n_full)]
    score_stage(chunks[0][0], chunks[0][1], True, 0)
    for n in range(1, len(chunks)):
        score_stage(chunks[n][0], chunks[n][1], False, n % 2)
        value_stage(chunks[n - 1][0], chunks[n - 1][1], (n - 1) % 2)
    value_stage(chunks[-1][0], chunks[-1][1], (len(chunks) - 1) % 2)
    _attn_finish(lam_init, lq1_ref, lk1_ref, lq2_ref, lk2_ref, subln_ref, o_ref, l_ref, acc_ref)


def _attn_scratch(tq, tk):
    return [
        pltpu.VMEM((2, tq, LANES), F32),
        pltpu.VMEM((2, tq, LANES), F32),
        pltpu.VMEM((2, tq, V_DIM), F32),
        pltpu.VMEM((2, tq, tk), BF16),
        pltpu.VMEM((2, tq, tk), BF16),
        pltpu.VMEM((2, tq, LANES), F32),
        pltpu.VMEM((2, tq, LANES), F32),
    ]


def _attention(qk, v, lq1, lk1, lq2, lk2, subln, layer, lam_init, batch, seq_len):
    n_heads = v.shape[1] // V_DIM
    tk = ATTN_CHUNK
    nq = seq_len // tk
    rem = seq_len - nq * tk
    qk3 = qk.reshape(batch, seq_len, qk.shape[1])
    v3 = v.reshape(batch, seq_len, v.shape[1])
    out_shape = jax.ShapeDtypeStruct((batch, seq_len, n_heads * V_DIM), BF16)

    def kv_param_specs(grid_rank):
        def at(*idx):
            return (lambda b, h, i: idx_of(b, h, idx)) if grid_rank == 3 else (lambda b, h: idx_of(b, h, idx))

        def idx_of(b, h, idx):
            return tuple(f(b, h) if callable(f) else f for f in idx)

        lam_spec = pl.BlockSpec((None, 1, HEAD_DIM), at(layer, 0, 0))
        return [pl.BlockSpec((None, seq_len, V_DIM), at(lambda b, h: b, 0, lambda b, h: n_heads + h)),
                pl.BlockSpec((None, seq_len, V_DIM), at(lambda b, h: b, 0, lambda b, h: h)),
                lam_spec, lam_spec, lam_spec, lam_spec,
                pl.BlockSpec((None, 1, V_DIM), at(layer, 0, 0))]

    out = pl.pallas_call(
        functools.partial(_attn_main_kernel, lam_init),
        grid=(batch, n_heads, nq),
        in_specs=[pl.BlockSpec((None, tk, V_DIM), lambda b, h, i: (b, i, h))] + kv_param_specs(3),
        out_specs=pl.BlockSpec((None, tk, V_DIM), lambda b, h, i: (b, i, h)),
        out_shape=out_shape,
        scratch_shapes=_attn_scratch(tk, tk),
        compiler_params=_params(("arbitrary", "arbitrary", "arbitrary")),
        name="attn_main",
    )(qk3, qk3, v3, lq1, lk1, lq2, lk2, subln)
    if rem:
        assert seq_len % rem == 0
        last = seq_len // rem - 1
        out = pl.pallas_call(
            functools.partial(_attn_tail_kernel, lam_init, tk),
            grid=(batch, n_heads),
            in_specs=([pl.BlockSpec((None, rem, V_DIM), lambda b, h: (b, last, h))] + kv_param_specs(2)
                      + [pl.BlockSpec(memory_space=pl.ANY)]),
            out_specs=pl.BlockSpec((None, rem, V_DIM), lambda b, h: (b, last, h)),
            out_shape=out_shape,
            scratch_shapes=_attn_scratch(rem, tk),
            input_output_aliases={8: 0},
            compiler_params=_params(("arbitrary", "arbitrary")),
            name="attn_tail",
        )(qk3, qk3, v3, lq1, lk1, lq2, lk2, subln, out)
    return out.reshape(batch * seq_len, n_heads * V_DIM)


def _lru_kernel(xr_ref, yr_ref, p_ref, w_ref, o_ref, halo_ref, carry_ref, a_ref, b_ref, h_ref):
    t = pl.program_id(2)
    tt, cw = xr_ref.shape
    halo = SUBLANES

    @pl.when(t == 0)
    def _():
        halo_ref[...] = jnp.zeros(halo_ref.shape, F32)
        carry_ref[...] = jnp.zeros(carry_ref.shape, F32)

    prm = p_ref[...]
    conv_b, ga_b, gx_b = prm[0:1], prm[1:2], prm[2:3]
    half_scale = (0.5 * LRU_C) * jax.nn.log_sigmoid(prm[3:4])
    row8 = lax.broadcasted_iota(jnp.int32, (halo, cw), 0)
    nl = cw // LANES
    rc = tt // LRU_ROW_CHUNKS
    ng = tt // SUBLANES
    assert ng % SUBLANES == 0 and rc % SUBLANES == 0

    def scan_rows(t0):
        s, g0 = divmod(t0, ng)
        return pl.ds(g0 * SUBLANES + s, SUBLANES, stride=SUBLANES)

    for c0 in range(0, tt, rc):
        x = xr_ref[c0:c0 + rc, :]
        prev = halo_ref[...] if c0 == 0 else xr_ref[c0 - halo:c0, :]
        xc = conv_b + x * prm[4:5]
        for j in range(1, CONV_W):
            xs = pltpu.roll(x, j, 0)
            head = jnp.where(row8 < j, pltpu.roll(prev, j, 0), xs[0:halo, :])
            xc = xc + jnp.concatenate([head, xs[halo:, :]], axis=0) * prm[4 + j:5 + j]

        gz = jnp.dot(xc.astype(BF16), w_ref[...], preferred_element_type=F32)
        tr = jnp.tanh(gz[:, :cw] + ga_b)
        ti = jnp.tanh(gz[:, cw:] + gx_b)
        log_a = tr * half_scale + half_scale
        a = jnp.exp(log_a)
        y = -jnp.tanh(log_a) * (a * a + 1.0)
        mult = jnp.where(y > 0.0, y * lax.rsqrt(y), 0.0)
        xh = 0.5 * xc
        bx = mult * (ti * xh + xh)
        for r0 in range(0, rc, SUBLANES):
            dst = scan_rows(c0 + r0)
            for k in range(nl):
                a_ref[k, dst, :] = a[r0:r0 + SUBLANES, k * LANES:(k + 1) * LANES]
                b_ref[k, dst, :] = bx[r0:r0 + SUBLANES, k * LANES:(k + 1) * LANES]
    halo_ref[...] = xr_ref[tt - halo:tt, :]

    def group(ref, g):
        rows = pl.ds(pl.multiple_of(g * SUBLANES, SUBLANES), SUBLANES)
        return jnp.concatenate([ref[k, rows, :] for k in range(nl)], axis=1)

    def put_group(ref, g, val):
        rows = pl.ds(pl.multiple_of(g * SUBLANES, SUBLANES), SUBLANES)
        for k in range(nl):
            ref[k, rows, :] = val[:, k * LANES:(k + 1) * LANES]

    def block_scan(g, state):
        h, cum = state
        ag = group(a_ref, g)
        h = ag * h + group(b_ref, g)
        cum = ag * cum
        put_group(h_ref, g, h)
        put_group(a_ref, g, cum)
        return h, cum

    h, cum = lax.fori_loop(0, ng, block_scan,
                           (jnp.zeros((SUBLANES, cw), F32), jnp.ones((SUBLANES, cw), F32)),
                           unroll=LRU_SCAN_UNROLL)
    e = carry_ref[...]
    entry = []
    for s in range(SUBLANES):
        entry.append(e)
        e = cum[s:s + 1, :] * e + h[s:s + 1, :]
    carry_ref[...] = e
    entry = jnp.concatenate(entry, axis=0)

    def block_fix(g, carry):
        put_group(h_ref, g, group(h_ref, g) + group(a_ref, g) * entry)
        return carry

    lax.fori_loop(0, ng, block_fix, 0, unroll=LRU_SCAN_UNROLL)

    for c0 in range(0, tt, rc):
        hs = jnp.concatenate(
            [jnp.concatenate([h_ref[k, scan_rows(c0 + r0), :] for k in range(nl)], axis=1)
             for r0 in range(0, rc, SUBLANES)], axis=0)
        o_ref[c0:c0 + rc, :] = (hs * jax.nn.gelu(yr_ref[c0:c0 + rc, :])).astype(o_ref.dtype)


def _lru(z_lru, prm, w_gate, layer, batch, seq_len):
    rows = z_lru.shape[0]
    width = z_lru.shape[1] // 2
    cw = LRU_SUPER
    ns = width // cw
    tt = SEQ_BLOCK
    nt = seq_len // tt
    return pl.pallas_call(
        _lru_kernel,
        grid=(batch, ns, nt),
        in_specs=[
            pl.BlockSpec((tt, cw), lambda b, s, t: (b * nt + t, s)),
            pl.BlockSpec((tt, cw), lambda b, s, t: (b * nt + t, ns + s)),
            pl.BlockSpec((None, 2 * CONV_W, cw), lambda b, s, t: (layer, 0, s)),
            pl.BlockSpec((None, None, cw, 2 * cw), lambda b, s, t: (layer, s, 0, 0)),
        ],
        out_specs=pl.BlockSpec((tt, cw), lambda b, s, t: (b * nt + t, s)),
        out_shape=jax.ShapeDtypeStruct((rows, width), BF16),
        scratch_shapes=[
            pltpu.VMEM((SUBLANES, cw), F32),
            pltpu.VMEM((1, cw), F32),
            pltpu.VMEM((cw // LANES, tt, LANES), F32),
            pltpu.VMEM((cw // LANES, tt, LANES), F32),
            pltpu.VMEM((cw // LANES, tt, LANES), F32),
        ],
        compiler_params=_params(("arbitrary", "arbitrary", "arbitrary")),
        name="rglru",
    )(z_lru, z_lru, prm, w_gate)


def _lru_gate_weights(ga_w, gx_w):
    depth, nb, bw, _ = ga_w.shape
    per = LRU_SUPER // bw
    ns = nb // per

    def dense(w):
        w = w.reshape(depth, ns, per, bw, bw)
        eye = jnp.eye(per, dtype=w.dtype)
        full = jnp.einsum("lspkj,pq->lspkqj", w, eye)
        return full.reshape(depth, ns, per * bw, per * bw)

    return (0.5 * jnp.concatenate([dense(ga_w), dense(gx_w)], axis=-1)).astype(BF16)


def _mix_kernel(oa_ref, ol_ref, wa_ref, wl_ref, ga_ref, gl_ref, o_ref):
    ya = jnp.dot(oa_ref[...], wa_ref[...], preferred_element_type=F32)
    yl = jnp.dot(ol_ref[...], wl_ref[...], preferred_element_type=F32)
    o_ref[...] = (ga_ref[...] * ya + gl_ref[...] * yl).astype(o_ref.dtype)


def _mix(oa, ol, w_ba, w_bl, gates, layer):
    rows, da = oa.shape
    dl = ol.shape[1]
    d = w_ba.shape[-1]
    tm = _pick(rows, (1056, 768, 384, 128))
    tn = _pick(d, (512, 256, 128))
    nb = d // tn
    return pl.pallas_call(
        _mix_kernel,
        grid=(rows // tm, nb),
        in_specs=[
            pl.BlockSpec((tm, da), lambda i, n: (i, 0)),
            pl.BlockSpec((tm, dl), lambda i, n: (i, 0)),
            pl.BlockSpec((None, da, tn), lambda i, n: (layer, 0, n)),
            pl.BlockSpec((None, dl, tn), lambda i, n: (layer, 0, n)),
            pl.BlockSpec((tm, tn), lambda i, n: (i, n)),
            pl.BlockSpec((tm, tn), lambda i, n: (i, nb + n)),
        ],
        out_specs=pl.BlockSpec((tm, tn), lambda i, n: (i, n)),
        out_shape=jax.ShapeDtypeStruct((rows, d), BF16),
        compiler_params=_params(("arbitrary", "arbitrary")),
        name="branch_mix",
    )(oa, ol, w_ba, w_bl, gates, gates)


def _out_kernel(y_ref, w_ref, h_ref, o_ref):
    o_ref[...] = h_ref[...] + jnp.dot(y_ref[...], w_ref[...], preferred_element_type=F32)


def _out_proj(y, w_o, h, layer):
    rows, d = h.shape
    tm = _pick(rows, (1056, 768, 384, 128))
    tn = _pick(d, (512, 256, 128))
    return pl.pallas_call(
        _out_kernel,
        grid=(rows // tm, d // tn),
        in_specs=[
            pl.BlockSpec((tm, y.shape[1]), lambda i, n: (i, 0)),
            pl.BlockSpec((None, y.shape[1], tn), lambda i, n: (layer, 0, n)),
            pl.BlockSpec((tm, tn), lambda i, n: (i, n)),
        ],
        out_specs=pl.BlockSpec((tm, tn), lambda i, n: (i, n)),
        out_shape=jax.ShapeDtypeStruct((rows, d), F32),
        compiler_params=_params(("arbitrary", "arbitrary")),
        name="out_proj",
    )(y, w_o, h)


def _final_norm_kernel(h_ref, next_ref, g_ref, o_ref):
    x = jnp.concatenate([h_ref[N_META:, :], next_ref[...]], axis=0)
    ms = jnp.mean(x * x, axis=-1, keepdims=True)
    o_ref[...] = x * lax.rsqrt(ms + EPS) * g_ref[...]


def _final_norm(h, gain, batch, seq_len, seq):
    d = h.shape[1]
    tb = _pick(seq, (512, 256, 128))
    assert tb % N_META == 0 and seq + N_META <= seq_len
    h3 = h.reshape(batch, seq_len, d)
    return pl.pallas_call(
        _final_norm_kernel,
        grid=(batch, seq // tb),
        in_specs=[pl.BlockSpec((None, tb, d), lambda b, j: (b, j, 0)),
                  pl.BlockSpec((None, N_META, d), lambda b, j: (b, (j + 1) * (tb // N_META), 0)),
                  pl.BlockSpec((1, d), lambda b, j: (0, 0))],
        out_specs=pl.BlockSpec((None, tb, d), lambda b, j: (b, j, 0)),
        out_shape=jax.ShapeDtypeStruct((batch, seq, d), F32),
        compiler_params=_params(("arbitrary", "arbitrary")),
        name="final_norm",
    )(h3, h3, gain)


def kernel(x, meta_tokens, norm_ffn1, ffn1_w_gate, ffn1_w_up, ffn1_w_down, norm_mix, w_in, b_gate, lambda_q1, lambda_k1, lambda_q2, lambda_k2, attn_subln, conv_w, conv_b, gate_x_w, gate_x_b, gate_a_w, gate_a_b, lru_a_param, w_branch_attn, w_branch_lru, w_out, norm_ffn2, ffn2_w_gate, ffn2_w_up, ffn2_w_down, final_norm):
    batch, seq, d = x.shape
    depth = w_in.shape[0]
    attn_v = w_branch_attn.shape[1]
    attn_qk = attn_v
    lru_w = w_branch_lru.shape[1]
    seq_real = N_META + seq
    tp = -(-seq_real // SEQ_BLOCK) * SEQ_BLOCK
    rows = batch * tp

    meta = jnp.broadcast_to(meta_tokens.astype(x.dtype)[None], (batch, N_META, d))
    h = jnp.concatenate([meta, x, jnp.zeros((batch, tp - seq_real, d), x.dtype)], axis=1)
    h = h.reshape(rows, d)

    def vec(p):
        return p[:, None, :]

    bf = lambda w: w.astype(BF16)
    f1g, f1u, f1d = bf(ffn1_w_gate), bf(ffn1_w_up), bf(ffn1_w_down)
    f2g, f2u, f2d = bf(ffn2_w_gate), bf(ffn2_w_up), bf(ffn2_w_down)
    w_in_b, w_ba, w_bl, w_o = bf(w_in), bf(w_branch_attn), bf(w_branch_lru), bf(w_out)
    w_lru_gate = _lru_gate_weights(gate_a_w, gate_x_w)
    lru_prm = jnp.concatenate([conv_b[:, None], 0.5 * gate_a_b[:, None], 0.5 * gate_x_b[:, None],
                               lru_a_param[:, None], conv_w], axis=1)
    rope = _rope_tables(tp)
    n1, nm, n2 = vec(norm_ffn1), vec(norm_mix), vec(norm_ffn2)
    bg = vec(b_gate)
    lq1, lk1, lq2, lk2 = vec(lambda_q1), vec(lambda_k1), vec(lambda_q2), vec(lambda_k2)
    subln = vec(attn_subln)

    c_v = 2 * attn_qk
    c_lru = c_v + attn_v
    c_gate = c_lru + 2 * lru_w
    for l in range(depth):
        lam_init = 0.8 - 0.6 * math.exp(-0.3 * l)
        h = _ffn(h, n1, f1g, f1u, f1d, l)
        xn = _norm(h, nm, l)
        qk = _proj(xn, w_in_b, l, 0, 2 * attn_qk, BF16, "rope", rope, seq_len=tp)
        v = _proj(xn, w_in_b, l, c_v, attn_v, BF16, "plain")
        z_lru = _proj(xn, w_in_b, l, c_lru, 2 * lru_w, F32, "plain")
        gates = _proj(xn, w_in_b, l, c_gate, 2 * d, F32, "gate", bg)
        o_attn = _attention(qk, v, lq1, lk1, lq2, lk2, subln, l, lam_init, batch, tp)
        o_lru = _lru(z_lru, lru_prm, w_lru_gate, l, batch, tp)
        y = _mix(o_attn, o_lru, w_ba, w_bl, gates, l)
        h = _out_proj(y, w_o, h, l)
        h = _ffn(h, n2, f2g, f2u, f2d, l)
    return _final_norm(h, final_norm[None, :], batch, tp, seq)
```

```python
import functools
import math

import jax
import jax.numpy as jnp
from jax import lax
from jax.experimental import pallas as pl
from jax.experimental.pallas import tpu as pltpu

N_META = 16
HEAD_DIM = 128
V_DIM = 2 * HEAD_DIM
ROT_DIM = HEAD_DIM // 4
ROPE_THETA = 500000.0
CONV_W = 4
LRU_C = 8.0
EPS = 1e-6

LANES = 128
SUBLANES = 8
BF16_ROWS = 16
CAST_COLS = 2048
SEQ_BLOCK = 384
ATTN_CHUNK = 2 * SEQ_BLOCK
ATTN_ROWS_PER_DOT = SEQ_BLOCK
LRU_SUPER = 640
LRU_ROW_CHUNKS = 2
LRU_SCAN_UNROLL = 8
VMEM_LIMIT_BYTES = 60 * 1024 * 1024
NEG_BIG = -1e30

BF16 = jnp.bfloat16
F32 = jnp.float32


def _pick(n, prefs):
    for p in prefs:
        if p <= n and n % p == 0:
            return p
    return n


def _params(semantics):
    return pltpu.CompilerParams(dimension_semantics=semantics, vmem_limit_bytes=VMEM_LIMIT_BYTES)


def _sigmoid(x):
    return 0.5 * jnp.tanh(0.5 * x) + 0.5


def _rms_norm_bf16(x, gain):
    ms = jnp.mean(x * x, axis=-1, keepdims=True)
    return (x * lax.rsqrt(ms + EPS) * gain).astype(BF16)


class _SideCast:
    def __init__(self, stacked, layer, n_steps, step_of):
        self.shapes = [w.shape[1:] for w in stacked]
        self.views = [w.reshape(w.shape[0], -1, CAST_COLS) for w in stacked]
        rows = [v.shape[1] for v in self.views]
        self.ok = all(r % n_steps == 0 and (r // n_steps) % BF16_ROWS == 0 for r in rows)
        blocks = [r // n_steps for r in rows]
        self.in_specs = [pl.BlockSpec((None, rb, CAST_COLS), lambda *g: (layer, step_of(*g), 0))
                         for rb in blocks]
        self.out_specs = [pl.BlockSpec((rb, CAST_COLS), lambda *g: (step_of(*g), 0)) for rb in blocks]
        self.out_shapes = [jax.ShapeDtypeStruct((r, CAST_COLS), BF16) for r in rows]
        self.n = len(stacked)

    @staticmethod
    def run(in_refs, out_refs):
        for src, dst in zip(in_refs, out_refs):
            dst[...] = src[...].astype(BF16)

    def finish(self, outs):
        return [o.reshape(s) for o, s in zip(outs, self.shapes)]


def _ffn_kernel(n_cast, h_ref, gain_ref, wg_ref, wu_ref, wd_ref, *rest):
    cast_in, o_ref = rest[:n_cast], rest[n_cast]
    cast_out, xn_ref = rest[n_cast + 1:2 * n_cast + 1], rest[-1]
    _SideCast.run(cast_in, cast_out)
    f = pl.program_id(1)

    @pl.when(f == 0)
    def _():
        x = h_ref[...]
        xn_ref[...] = _rms_norm_bf16(x, gain_ref[...])
        o_ref[...] = x

    xn = xn_ref[...]
    g = jnp.dot(xn, wg_ref[...], preferred_element_type=F32)
    u = jnp.dot(xn, wu_ref[...], preferred_element_type=F32)
    a = (0.5 * (g * _sigmoid(g)) * u).astype(BF16)
    o_ref[...] += jnp.dot(a, wd_ref[...], preferred_element_type=F32)


def _ffn(h, gain, layer, weights, cast_next=None):
    wg, wu, wd = weights
    rows, d = h.shape
    ff = wg.shape[-1]
    tm = _pick(rows, (1056, 768, 512, 384, 256, 128))
    tf = _pick(ff, (512, 256, 128))
    nf = ff // tf
    cast = None
    if cast_next is not None:
        cast = _SideCast(cast_next[0], cast_next[1], (rows // tm) * nf, lambda i, f: i * nf + f)
        if not cast.ok:
            cast = None
    n_cast = cast.n if cast else 0
    outs = pl.pallas_call(
        functools.partial(_ffn_kernel, n_cast),
        grid=(rows // tm, nf),
        in_specs=[
            pl.BlockSpec((tm, d), lambda i, f: (i, 0)),
            pl.BlockSpec((None, 1, d), lambda i, f: (layer, 0, 0)),
            pl.BlockSpec((d, tf), lambda i, f: (0, f)),
            pl.BlockSpec((d, tf), lambda i, f: (0, f)),
            pl.BlockSpec((tf, d), lambda i, f: (f, 0)),
        ] + (cast.in_specs if cast else []),
        out_specs=[pl.BlockSpec((tm, d), lambda i, f: (i, 0))] + (cast.out_specs if cast else []),
        out_shape=[jax.ShapeDtypeStruct((rows, d), F32)] + (cast.out_shapes if cast else []),
        scratch_shapes=[pltpu.VMEM((tm, d), BF16)],
        compiler_params=_params(("arbitrary", "arbitrary")),
        name="ffn",
    )(h, gain, wg, wu, wd, *(cast.views if cast else []))
    if cast:
        return outs[0], cast.finish(outs[1:])
    fallback = None if cast_next is None else [w[cast_next[1]].astype(BF16) for w in cast_next[0]]
    return outs[0], fallback


def _norm_kernel(h_ref, gain_ref, o_ref):
    o_ref[...] = _rms_norm_bf16(h_ref[...], gain_ref[...])


def _norm(h, gain, layer):
    rows, d = h.shape
    tm = _pick(rows, (1056, 768, 384, 128))
    return pl.pallas_call(
        _norm_kernel,
        grid=(rows // tm,),
        in_specs=[pl.BlockSpec((tm, d), lambda i: (i, 0)),
                  pl.BlockSpec((None, 1, d), lambda i: (layer, 0, 0))],
        out_specs=pl.BlockSpec((tm, d), lambda i: (i, 0)),
        out_shape=jax.ShapeDtypeStruct((rows, d), BF16),
        compiler_params=_params(("arbitrary",)),
        name="mix_norm",
    )(h, gain)


def _proj_kernel(kind, xn_ref, w_ref, *rest):
    o_ref = rest[-1]
    tm, tn = o_ref.shape
    chunks = {"rope": 4, "gate": 2, "plain": 1}[kind]
    cm = tm // chunks
    for r in range(chunks):
        rows = slice(r * cm, (r + 1) * cm)
        z = jnp.dot(xn_ref[rows, :], w_ref[...], preferred_element_type=F32)
        if kind == "rope":
            tab_ref = rest[0]
            reps = tn // LANES
            cos = jnp.concatenate([tab_ref[0, rows, :]] * reps, axis=1)
            sin_lo = jnp.concatenate([tab_ref[1, rows, :]] * reps, axis=1)
            sin_hi = jnp.concatenate([tab_ref[2, rows, :]] * reps, axis=1)
            half = ROT_DIM // 2
            z = z * cos + pltpu.roll(z, half, 1) * sin_lo + pltpu.roll(z, tn - half, 1) * sin_hi
        elif kind == "gate":
            z = _sigmoid(z + rest[0][...])
        o_ref[rows, :] = z.astype(o_ref.dtype)


def _proj(xn, w, layer, col0, ncols, out_dtype, kind, extra=None, seq_len=None):
    rows, d = xn.shape
    tm = _pick(seq_len, (2112, 1408, 768, 384, 128)) if seq_len else _pick(rows, (2112, 1408, 768, 384, 128))
    tn = _pick(ncols // 2 if kind == "rope" else ncols, (512, 256, 128))
    assert col0 % tn == 0
    cb0 = col0 // tn
    in_specs = [
        pl.BlockSpec((tm, d), lambda i, n: (i, 0)),
        pl.BlockSpec((d, tn), lambda i, n: (0, cb0 + n)),
    ]
    args = [xn, w]
    if kind == "rope":
        tiles_per_seq = seq_len // tm
        q_blocks = ncols // 2 // tn
        in_specs.append(pl.BlockSpec((None, 3, tm, LANES),
                                     lambda i, n: (n // q_blocks, 0, i % tiles_per_seq, 0)))
        args.append(extra)
    elif kind == "gate":
        in_specs.append(pl.BlockSpec((None, 1, tn), lambda i, n: (layer, 0, n)))
        args.append(extra)
    return pl.pallas_call(
        functools.partial(_proj_kernel, kind),
        grid=(rows // tm, ncols // tn),
        in_specs=in_specs,
        out_specs=pl.BlockSpec((tm, tn), lambda i, n: (i, n)),
        out_shape=jax.ShapeDtypeStruct((rows, ncols), out_dtype),
        compiler_params=_params(("arbitrary", "arbitrary")),
        name="proj_" + kind,
    )(*args)


def _rope_tables(seq_len):
    half = ROT_DIM // 2
    inv_freq = ROPE_THETA ** (-jnp.arange(0, ROT_DIM, 2, dtype=F32) / ROT_DIM)
    ang = jnp.arange(seq_len, dtype=F32)[:, None] * inv_freq[None, :]
    cos, sin = jnp.cos(ang), jnp.sin(ang)
    ones = jnp.ones((seq_len, LANES - ROT_DIM), F32)
    zeros = jnp.zeros((seq_len, LANES - ROT_DIM), F32)
    zh = jnp.zeros((seq_len, half), F32)
    c = jnp.concatenate([cos, cos, ones], axis=1)
    s_lo = jnp.concatenate([zh, sin, zeros], axis=1)
    s_hi = jnp.concatenate([-sin, zh, zeros], axis=1)
    k_tab = jnp.stack([c, s_lo, s_hi])
    return jnp.stack([k_tab * ((HEAD_DIM ** -0.5) * math.log2(math.e)), k_tab])


def _attn_stages(k_ref, v_ref, qs, m_ref, l_ref, acc_ref, p_refs, a_refs):
    tq = qs[0].shape[0]
    rows_per_dot = min(tq, ATTN_ROWS_PER_DOT)

    def score_stage(start, width, masked, slot):
        kb = k_ref[pl.ds(start, width), :]
        for c in range(2):
            kc = kb[:, c * HEAD_DIM:(c + 1) * HEAD_DIM]
            for r0 in range(0, tq, rows_per_dot):
                rows = slice(r0, r0 + rows_per_dot)
                kw = min(width, r0 + rows_per_dot) if masked else width
                s = lax.dot_general(qs[c][rows], kc[0:kw], (((1,), (1,)), ((), ())),
                                    preferred_element_type=F32)
                if masked:
                    row = lax.broadcasted_iota(jnp.int32, s.shape, 0) + r0
                    col = lax.broadcasted_iota(jnp.int32, s.shape, 1)
                    s = jnp.where(col <= row, s, NEG_BIG)
                m_old = m_ref[c, rows, :]
                m_new = jnp.maximum(m_old, jnp.max(s, axis=-1, keepdims=True))
                alpha = jnp.exp2(m_old - m_new)
                p = jnp.exp2(s - jnp.tile(m_new, (1, kw // LANES)))
                l_ref[c, rows, :] = alpha * l_ref[c, rows, :] + jnp.sum(p, axis=-1, keepdims=True)
                m_ref[c, rows, :] = m_new
                a_refs[slot][c, rows, :] = alpha
                p_refs[slot][c, rows, 0:kw] = p.astype(BF16)
                if kw < width:
                    p_refs[slot][c, rows, kw:width] = jnp.zeros((rows_per_dot, width - kw), BF16)

    def value_stage(start, width, slot):
        vb = v_ref[pl.ds(start, width), :]
        for c in range(2):
            for r0 in range(0, tq, rows_per_dot):
                rows = slice(r0, r0 + rows_per_dot)
                pv = jnp.dot(p_refs[slot][c, rows, 0:width], vb, preferred_element_type=F32)
                acc_ref[c, rows, :] = (jnp.tile(a_refs[slot][c, rows, :], (1, V_DIM // LANES))
                                       * acc_ref[c, rows, :] + pv)

    return score_stage, value_stage


def _attn_init(m_ref, l_ref, acc_ref):
    m_ref[...] = jnp.full(m_ref.shape, NEG_BIG, F32)
    l_ref[...] = jnp.zeros(l_ref.shape, F32)
    acc_ref[...] = jnp.zeros(acc_ref.shape, F32)


def _attn_finish(lam_init, lq1_ref, lk1_ref, lq2_ref, lk2_ref, subln_ref, o_ref, l_ref, acc_ref):
    lam = (jnp.exp(jnp.sum(lq1_ref[...] * lk1_ref[...], keepdims=True))
           - jnp.exp(jnp.sum(lq2_ref[...] * lk2_ref[...], keepdims=True)) + lam_init)
    reps = V_DIM // LANES
    o = (acc_ref[0] * jnp.tile(1.0 / l_ref[0], (1, reps))
         - lam * (acc_ref[1] * jnp.tile(1.0 / l_ref[1], (1, reps))))
    o = o * lax.rsqrt(jnp.mean(o * o, axis=-1, keepdims=True) + EPS)
    o = o * subln_ref[...] * (1.0 - lam_init)
    o_ref[...] = o.astype(o_ref.dtype)


def _attn_main_kernel(lam_init, n_cast, q_ref, k_ref, v_ref, lq1_ref, lk1_ref, lq2_ref, lk2_ref,
                      subln_ref, *rest):
    cast_in, o_ref, cast_out = rest[:n_cast], rest[n_cast], rest[n_cast + 1:2 * n_cast + 1]
    m_ref, l_ref, acc_ref, p0_ref, p1_ref, a0_ref, a1_ref = rest[2 * n_cast + 1:]
    _SideCast.run(cast_in, cast_out)
    i = pl.program_id(2)
    tq = q_ref.shape[0]
    q = q_ref[...]
    score_stage, value_stage = _attn_stages(k_ref, v_ref, (q[:, :HEAD_DIM], q[:, HEAD_DIM:]),
                                            m_ref, l_ref, acc_ref, (p0_ref, p1_ref), (a0_ref, a1_ref))
    _attn_init(m_ref, l_ref, acc_ref)
    score_stage(pl.multiple_of(i * tq, tq), tq, True, 0)

    def chunk_start(n):
        return pl.multiple_of(jnp.where(n == 0, i, n - 1) * tq, tq)

    def body(n, carry):
        for slot in range(2):
            @pl.when(n % 2 == slot)
            def _():
                score_stage(chunk_start(n), tq, False, slot)
                value_stage(chunk_start(n - 1), tq, 1 - slot)
        return carry

    lax.fori_loop(1, i + 1, body, 0)
    for slot in range(2):
        @pl.when(i % 2 == slot)
        def _():
            value_stage(chunk_start(i), tq, slot)
    _attn_finish(lam_init, lq1_ref, lk1_ref, lq2_ref, lk2_ref, subln_ref, o_ref, l_ref, acc_ref)


def _attn_tail_kernel(lam_init, tk, q_ref, k_ref, v_ref, lq1_ref, lk1_ref, lq2_ref, lk2_ref, subln_ref,
                      o_in_ref, o_ref, m_ref, l_ref, acc_ref, p0_ref, p1_ref, a0_ref, a1_ref):
    del o_in_ref
    tq = q_ref.shape[0]
    n_full = (k_ref.shape[0] - tq) // tk
    q = q_ref[...]
    score_stage, value_stage = _attn_stages(k_ref, v_ref, (q[:, :HEAD_DIM], q[:, HEAD_DIM:]),
                                            m_ref, l_ref, acc_ref, (p0_ref, p1_ref), (a0_ref, a1_ref))
    _attn_init(m_ref, l_ref, acc_ref)
    chunks = [(n_full * tk, tq)] + [(j * tk, tk) for j in range(n_full)]
    score_stage(chunks[0][0], chunks[0][1], True, 0)
    for n in range(1, len(chunks)):
        score_stage(chunks[n][0], chunks[n][1], False, n % 2)
        value_stage(chunks[n - 1][0], chunks[n - 1][1], (n - 1) % 2)
    value_stage(chunks[-1][0], chunks[-1][1], (len(chunks) - 1) % 2)
    _attn_finish(lam_init, lq1_ref, lk1_ref, lq2_ref, lk2_ref, subln_ref, o_ref, l_ref, acc_ref)


def _attn_scratch(tq, tk):
    return [
        pltpu.VMEM((2, tq, LANES), F32),
        pltpu.VMEM((2, tq, LANES), F32),
        pltpu.VMEM((2, tq, V_DIM), F32),
        pltpu.VMEM((2, tq, tk), BF16),
        pltpu.VMEM((2, tq, tk), BF16),
        pltpu.VMEM((2, tq, LANES), F32),
        pltpu.VMEM((2, tq, LANES), F32),
    ]


def _attention(qk, v, lq1, lk1, lq2, lk2, subln, layer, lam_init, batch, seq_len, cast_next=None):
    n_heads = v.shape[1] // V_DIM
    tk = ATTN_CHUNK
    nq = seq_len // tk
    rem = seq_len - nq * tk
    qk3 = qk.reshape(batch, seq_len, qk.shape[1])
    v3 = v.reshape(batch, seq_len, v.shape[1])
    out_shape = jax.ShapeDtypeStruct((batch, seq_len, n_heads * V_DIM), BF16)

    def kv_param_specs(grid_rank):
        def at(*idx):
            return (lambda b, h, i: idx_of(b, h, idx)) if grid_rank == 3 else (lambda b, h: idx_of(b, h, idx))

        def idx_of(b, h, idx):
            return tuple(f(b, h) if callable(f) else f for f in idx)

        lam_spec = pl.BlockSpec((None, 1, HEAD_DIM), at(layer, 0, 0))
        return [pl.BlockSpec((None, seq_len, V_DIM), at(lambda b, h: b, 0, lambda b, h: n_heads + h)),
                pl.BlockSpec((None, seq_len, V_DIM), at(lambda b, h: b, 0, lambda b, h: h)),
                lam_spec, lam_spec, lam_spec, lam_spec,
                pl.BlockSpec((None, 1, V_DIM), at(layer, 0, 0))]

    cast = None
    if cast_next is not None:
        cast = _SideCast(cast_next[0], cast_next[1], batch * n_heads * nq,
                         lambda b, h, i: (b * n_heads + h) * nq + i)
        if not cast.ok:
            cast = None
    outs = pl.pallas_call(
        functools.partial(_attn_main_kernel, lam_init, cast.n if cast else 0),
        grid=(batch, n_heads, nq),
        in_specs=([pl.BlockSpec((None, tk, V_DIM), lambda b, h, i: (b, i, h))] + kv_param_specs(3)
                  + (cast.in_specs if cast else [])),
        out_specs=[pl.BlockSpec((None, tk, V_DIM), lambda b, h, i: (b, i, h))] + (cast.out_specs if cast else []),
        out_shape=[out_shape] + (cast.out_shapes if cast else []),
        scratch_shapes=_attn_scratch(tk, tk),
        compiler_params=_params(("arbitrary", "arbitrary", "arbitrary")),
        name="attn_main",
    )(qk3, qk3, v3, lq1, lk1, lq2, lk2, subln, *(cast.views if cast else []))
    out = outs[0]
    if cast:
        cast_out = cast.finish(outs[1:])
    else:
        cast_out = None if cast_next is None else [w[cast_next[1]].astype(BF16) for w in cast_next[0]]
    if rem:
        assert seq_len % rem == 0
        last = seq_len // rem - 1
        out = pl.pallas_call(
            functools.partial(_attn_tail_kernel, lam_init, tk),
            grid=(batch, n_heads),
            in_specs=([pl.BlockSpec((None, rem, V_DIM), lambda b, h: (b, last, h))] + kv_param_specs(2)
                      + [pl.BlockSpec(memory_space=pl.ANY)]),
            out_specs=pl.BlockSpec((None, rem, V_DIM), lambda b, h: (b, last, h)),
            out_shape=out_shape,
            scratch_shapes=_attn_scratch(rem, tk),
            input_output_aliases={8: 0},
            compiler_params=_params(("arbitrary", "arbitrary")),
            name="attn_tail",
        )(qk3, qk3, v3, lq1, lk1, lq2, lk2, subln, out)
    return out.reshape(batch * seq_len, n_heads * V_DIM), cast_out


def _lru_kernel(xr_ref, yr_ref, p_ref, w_ref, o_ref, halo_ref, carry_ref, a_ref, b_ref, h_ref):
    t = pl.program_id(2)
    tt, cw = xr_ref.shape
    halo = SUBLANES

    @pl.when(t == 0)
    def _():
        halo_ref[...] = jnp.zeros(halo_ref.shape, F32)
        carry_ref[...] = jnp.zeros(carry_ref.shape, F32)

    prm = p_ref[...]
    conv_b, ga_b, gx_b = prm[0:1], prm[1:2], prm[2:3]
    half_scale = (0.5 * LRU_C) * jax.nn.log_sigmoid(prm[3:4])
    row8 = lax.broadcasted_iota(jnp.int32, (halo, cw), 0)
    nl = cw // LANES
    rc = tt // LRU_ROW_CHUNKS
    ng = tt // SUBLANES
    assert ng % SUBLANES == 0 and rc % SUBLANES == 0

    def scan_rows(t0):
        s, g0 = divmod(t0, ng)
        return pl.ds(g0 * SUBLANES + s, SUBLANES, stride=SUBLANES)

    for c0 in range(0, tt, rc):
        x = xr_ref[c0:c0 + rc, :]
        prev = halo_ref[...] if c0 == 0 else xr_ref[c0 - halo:c0, :]
        xc = conv_b + x * prm[4:5]
        for j in range(1, CONV_W):
            xs = pltpu.roll(x, j, 0)
            head = jnp.where(row8 < j, pltpu.roll(prev, j, 0), xs[0:halo, :])
            xc = xc + jnp.concatenate([head, xs[halo:, :]], axis=0) * prm[4 + j:5 + j]

        gz = jnp.dot(xc.astype(BF16), w_ref[...], preferred_element_type=F32)
        tr = jnp.tanh(gz[:, :cw] + ga_b)
        ti = jnp.tanh(gz[:, cw:] + gx_b)
        log_a = tr * half_scale + half_scale
        a = jnp.exp(log_a)
        y = -jnp.tanh(log_a) * (a * a + 1.0)
        mult = jnp.where(y > 0.0, y * lax.rsqrt(y), 0.0)
        xh = 0.5 * xc
        bx = mult * (ti * xh + xh)
        for r0 in range(0, rc, SUBLANES):
            dst = scan_rows(c0 + r0)
            for k in range(nl):
                a_ref[k, dst, :] = a[r0:r0 + SUBLANES, k * LANES:(k + 1) * LANES]
                b_ref[k, dst, :] = bx[r0:r0 + SUBLANES, k * LANES:(k + 1) * LANES]
    halo_ref[...] = xr_ref[tt - halo:tt, :]

    def group(ref, g):
        rows = pl.ds(pl.multiple_of(g * SUBLANES, SUBLANES), SUBLANES)
        return jnp.concatenate([ref[k, rows, :] for k in range(nl)], axis=1)

    def put_group(ref, g, val):
        rows = pl.ds(pl.multiple_of(g * SUBLANES, SUBLANES), SUBLANES)
        for k in range(nl):
            ref[k, rows, :] = val[:, k * LANES:(k + 1) * LANES]

    def block_scan(g, state):
        h, cum = state
        ag = group(a_ref, g)
        h = ag * h + group(b_ref, g)
        cum = ag * cum
        put_group(h_ref, g, h)
        put_group(a_ref, g, cum)
        return h, cum

    h, cum = lax.fori_loop(0, ng, block_scan,
                           (jnp.zeros((SUBLANES, cw), F32), jnp.ones((SUBLANES, cw), F32)),
                           unroll=LRU_SCAN_UNROLL)
    e = carry_ref[...]
    entry = []
    for s in range(SUBLANES):
        entry.append(e)
        e = cum[s:s + 1, :] * e + h[s:s + 1, :]
    carry_ref[...] = e
    entry = jnp.concatenate(entry, axis=0)

    def block_fix(g, carry):
        put_group(h_ref, g, group(h_ref, g) + group(a_ref, g) * entry)
        return carry

    lax.fori_loop(0, ng, block_fix, 0, unroll=LRU_SCAN_UNROLL)

    for c0 in range(0, tt, rc):
        hs = jnp.concatenate(
            [jnp.concatenate([h_ref[k, scan_rows(c0 + r0), :] for k in range(nl)], axis=1)
             for r0 in range(0, rc, SUBLANES)], axis=0)
        o_ref[c0:c0 + rc, :] = (hs * jax.nn.gelu(yr_ref[c0:c0 + rc, :])).astype(o_ref.dtype)


def _lru(z_lru, prm, w_gate, layer, batch, seq_len):
    rows = z_lru.shape[0]
    width = z_lru.shape[1] // 2
    cw = LRU_SUPER
    ns = width // cw
    tt = SEQ_BLOCK
    nt = seq_len // tt
    return pl.pallas_call(
        _lru_kernel,
        grid=(batch, ns, nt),
        in_specs=[
            pl.BlockSpec((tt, cw), lambda b, s, t: (b * nt + t, s)),
            pl.BlockSpec((tt, cw), lambda b, s, t: (b * nt + t, ns + s)),
            pl.BlockSpec((None, 2 * CONV_W, cw), lambda b, s, t: (layer, 0, s)),
            pl.BlockSpec((None, None, cw, 2 * cw), lambda b, s, t: (layer, s, 0, 0)),
        ],
        out_specs=pl.BlockSpec((tt, cw), lambda b, s, t: (b * nt + t, s)),
        out_shape=jax.ShapeDtypeStruct((rows, width), BF16),
        scratch_shapes=[
            pltpu.VMEM((SUBLANES, cw), F32),
            pltpu.VMEM((1, cw), F32),
            pltpu.VMEM((cw // LANES, tt, LANES), F32),
            pltpu.VMEM((cw // LANES, tt, LANES), F32),
            pltpu.VMEM((cw // LANES, tt, LANES), F32),
        ],
        compiler_params=_params(("arbitrary", "arbitrary", "arbitrary")),
        name="rglru",
    )(z_lru, z_lru, prm, w_gate)


def _lru_gate_weights(ga_w, gx_w):
    depth, nb, bw, _ = ga_w.shape
    per = LRU_SUPER // bw
    ns = nb // per

    def dense(w):
        w = w.reshape(depth, ns, per, bw, bw)
        eye = jnp.eye(per, dtype=w.dtype)
        full = jnp.einsum("lspkj,pq->lspkqj", w, eye)
        return full.reshape(depth, ns, per * bw, per * bw)

    return (0.5 * jnp.concatenate([dense(ga_w), dense(gx_w)], axis=-1)).astype(BF16)


def _mix_kernel(oa_ref, ol_ref, wa_ref, wl_ref, ga_ref, gl_ref, o_ref):
    ya = jnp.dot(oa_ref[...], wa_ref[...], preferred_element_type=F32)
    yl = jnp.dot(ol_ref[...], wl_ref[...], preferred_element_type=F32)
    o_ref[...] = (ga_ref[...] * ya + gl_ref[...] * yl).astype(o_ref.dtype)


def _mix(oa, ol, w_ba, w_bl, gates, layer):
    rows, da = oa.shape
    dl = ol.shape[1]
    d = w_ba.shape[-1]
    tm = _pick(rows, (1408, 1056, 768, 384, 128))
    tn = _pick(d, (512, 256, 128))
    nb = d // tn
    return pl.pallas_call(
        _mix_kernel,
        grid=(rows // tm, nb),
        in_specs=[
            pl.BlockSpec((tm, da), lambda i, n: (i, 0)),
            pl.BlockSpec((tm, dl), lambda i, n: (i, 0)),
            pl.BlockSpec((None, da, tn), lambda i, n: (layer, 0, n)),
            pl.BlockSpec((None, dl, tn), lambda i, n: (layer, 0, n)),
            pl.BlockSpec((tm, tn), lambda i, n: (i, n)),
            pl.BlockSpec((tm, tn), lambda i, n: (i, nb + n)),
        ],
        out_specs=pl.BlockSpec((tm, tn), lambda i, n: (i, n)),
        out_shape=jax.ShapeDtypeStruct((rows, d), BF16),
        compiler_params=_params(("arbitrary", "arbitrary")),
        name="branch_mix",
    )(oa, ol, w_ba, w_bl, gates, gates)


def _out_kernel(y_ref, w_ref, h_ref, o_ref):
    o_ref[...] = h_ref[...] + jnp.dot(y_ref[...], w_ref[...], preferred_element_type=F32)


def _out_proj(y, w_o, h, layer):
    rows, d = h.shape
    tm = _pick(rows, (2112, 1056, 768, 384, 128))
    tn = _pick(d, (512, 256, 128))
    return pl.pallas_call(
        _out_kernel,
        grid=(rows // tm, d // tn),
        in_specs=[
            pl.BlockSpec((tm, y.shape[1]), lambda i, n: (i, 0)),
            pl.BlockSpec((None, y.shape[1], tn), lambda i, n: (layer, 0, n)),
            pl.BlockSpec((tm, tn), lambda i, n: (i, n)),
        ],
        out_specs=pl.BlockSpec((tm, tn), lambda i, n: (i, n)),
        out_shape=jax.ShapeDtypeStruct((rows, d), F32),
        compiler_params=_params(("arbitrary", "arbitrary")),
        name="out_proj",
    )(y, w_o, h)


def _final_norm_kernel(h_ref, next_ref, g_ref, o_ref):
    x = jnp.concatenate([h_ref[N_META:, :], next_ref[...]], axis=0)
    ms = jnp.mean(x * x, axis=-1, keepdims=True)
    o_ref[...] = x * lax.rsqrt(ms + EPS) * g_ref[...]


def _final_norm(h, gain, batch, seq_len, seq):
    d = h.shape[1]
    tb = _pick(seq, (512, 256, 128))
    assert tb % N_META == 0 and seq + N_META <= seq_len
    h3 = h.reshape(batch, seq_len, d)
    return pl.pallas_call(
        _final_norm_kernel,
        grid=(batch, seq // tb),
        in_specs=[pl.BlockSpec((None, tb, d), lambda b, j: (b, j, 0)),
                  pl.BlockSpec((None, N_META, d), lambda b, j: (b, (j + 1) * (tb // N_META), 0)),
                  pl.BlockSpec((1, d), lambda b, j: (0, 0))],
        out_specs=pl.BlockSpec((None, tb, d), lambda b, j: (b, j, 0)),
        out_shape=jax.ShapeDtypeStruct((batch, seq, d), F32),
        compiler_params=_params(("arbitrary", "arbitrary")),
        name="final_norm",
    )(h3, h3, gain)


def kernel(x, meta_tokens, norm_ffn1, ffn1_w_gate, ffn1_w_up, ffn1_w_down, norm_mix, w_in, b_gate, lambda_q1, lambda_k1, lambda_q2, lambda_k2, attn_subln, conv_w, conv_b, gate_x_w, gate_x_b, gate_a_w, gate_a_b, lru_a_param, w_branch_attn, w_branch_lru, w_out, norm_ffn2, ffn2_w_gate, ffn2_w_up, ffn2_w_down, final_norm):
    batch, seq, d = x.shape
    depth = w_in.shape[0]
    attn_v = w_branch_attn.shape[1]
    attn_qk = attn_v
    lru_w = w_branch_lru.shape[1]
    seq_real = N_META + seq
    tp = -(-seq_real // SEQ_BLOCK) * SEQ_BLOCK
    rows = batch * tp

    meta = jnp.broadcast_to(meta_tokens.astype(x.dtype)[None], (batch, N_META, d))
    h = jnp.concatenate([meta, x, jnp.zeros((batch, tp - seq_real, d), x.dtype)], axis=1)
    h = h.reshape(rows, d)

    def vec(p):
        return p[:, None, :]

    bf = lambda w: w.astype(BF16)
    ffn1_f32 = (ffn1_w_gate, ffn1_w_up, ffn1_w_down)
    ffn2_f32 = (ffn2_w_gate, ffn2_w_up, ffn2_w_down)
    f1 = [bf(w[0]) for w in ffn1_f32]
    f2 = [bf(w[0]) for w in ffn2_f32]
    w_in_l = bf(w_in[0])
    w_ba, w_bl, w_o = bf(w_branch_attn), bf(w_branch_lru), bf(w_out)
    w_lru_gate = _lru_gate_weights(gate_a_w, gate_x_w)
    lru_prm = jnp.concatenate([conv_b[:, None], 0.5 * gate_a_b[:, None], 0.5 * gate_x_b[:, None],
                               lru_a_param[:, None], conv_w], axis=1)
    rope = _rope_tables(tp)
    n1, nm, n2 = vec(norm_ffn1), vec(norm_mix), vec(norm_ffn2)
    bg = vec(b_gate)
    lq1, lk1, lq2, lk2 = vec(lambda_q1), vec(lambda_k1), vec(lambda_q2), vec(lambda_k2)
    subln = vec(attn_subln)

    c_v = 2 * attn_qk
    c_lru = c_v + attn_v
    c_gate = c_lru + 2 * lru_w
    for l in range(depth):
        lam_init = 0.8 - 0.6 * math.exp(-0.3 * l)
        more = l + 1 < depth
        h, f1 = _ffn(h, n1, l, f1, (ffn1_f32, l + 1) if more else None)
        xn = _norm(h, nm, l)
        qk = _proj(xn, w_in_l, l, 0, 2 * attn_qk, BF16, "rope", rope, seq_len=tp)
        v = _proj(xn, w_in_l, l, c_v, attn_v, BF16, "plain")
        z_lru = _proj(xn, w_in_l, l, c_lru, 2 * lru_w, F32, "plain")
        gates = _proj(xn, w_in_l, l, c_gate, 2 * d, F32, "gate", bg)
        o_attn, w_in_next = _attention(qk, v, lq1, lk1, lq2, lk2, subln, l, lam_init, batch, tp,
                                       ((w_in,), l + 1) if more else None)
        if more:
            w_in_l = w_in_next[0]
        o_lru = _lru(z_lru, lru_prm, w_lru_gate, l, batch, tp)
        y = _mix(o_attn, o_lru, w_ba, w_bl, gates, l)
        h = _out_proj(y, w_o, h, l)
        h, f2 = _ffn(h, n2, l, f2, (ffn2_f32, l + 1) if more else None)
    return _final_norm(h, final_norm[None, :], batch, tp, seq)
```

```python
import functools
import math

import jax
import jax.numpy as jnp
from jax import lax
from jax.experimental import pallas as pl
from jax.experimental.pallas import tpu as pltpu

N_META = 16
HEAD_DIM = 128
V_DIM = 2 * HEAD_DIM
ROT_DIM = HEAD_DIM // 4
ROPE_THETA = 500000.0
CONV_W = 4
LRU_C = 8.0
EPS = 1e-6

LANES = 128
SUBLANES = 8
BF16_ROWS = 16
SEQ_BLOCK = 384
ATTN_CHUNK = 2 * SEQ_BLOCK
ATTN_ROWS_PER_DOT = SEQ_BLOCK
LRU_SUPER = 640
LRU_ROW_CHUNKS = 2
LRU_SCAN_UNROLL = 8
VMEM_LIMIT_BYTES = 60 * 1024 * 1024
NEG_BIG = -1e30

BF16 = jnp.bfloat16
F32 = jnp.float32


def _pick(n, prefs):
    for p in prefs:
        if p <= n and n % p == 0:
            return p
    return n


def _params(semantics):
    return pltpu.CompilerParams(dimension_semantics=semantics, vmem_limit_bytes=VMEM_LIMIT_BYTES)


def _sigmoid(x):
    return 0.5 * jnp.tanh(0.5 * x) + 0.5


def _rms_norm_bf16(x, gain):
    ms = jnp.mean(x * x, axis=-1, keepdims=True)
    return (x * lax.rsqrt(ms + EPS) * gain).astype(BF16)


class _SideCast:
    def __init__(self, stacked, layer, n_steps, step_of):
        self.views = list(stacked)
        self.n = len(stacked)
        self.in_specs, self.out_specs, self.out_shapes = [], [], []
        self.ok = True
        for w in stacked:
            _, rows, cols = w.shape
            fits = [rb for rb in range(BF16_ROWS, rows + 1, BF16_ROWS)
                    if rows % rb == 0 and rb * n_steps >= rows]
            if not fits:
                self.ok = False
                continue
            rb = fits[0]
            last = rows // rb - 1
            self.in_specs.append(pl.BlockSpec(
                (None, rb, cols), lambda *g, last=last: (layer, jnp.minimum(step_of(*g), last), 0)))
            self.out_specs.append(pl.BlockSpec(
                (rb, cols), lambda *g, last=last: (jnp.minimum(step_of(*g), last), 0)))
            self.out_shapes.append(jax.ShapeDtypeStruct((rows, cols), BF16))

    @staticmethod
    def run(in_refs, out_refs):
        for src, dst in zip(in_refs, out_refs):
            dst[...] = src[...].astype(BF16)

    def finish(self, outs):
        return list(outs)


def _ffn_kernel(n_cast, h_ref, gain_ref, wg_ref, wu_ref, wd_ref, *rest):
    cast_in, o_ref = rest[:n_cast], rest[n_cast]
    cast_out, xn_ref = rest[n_cast + 1:2 * n_cast + 1], rest[-1]
    _SideCast.run(cast_in, cast_out)
    f = pl.program_id(1)

    @pl.when(f == 0)
    def _():
        x = h_ref[...]
        xn_ref[...] = _rms_norm_bf16(x, gain_ref[...])
        o_ref[...] = x

    xn = xn_ref[...]
    g = jnp.dot(xn, wg_ref[...], preferred_element_type=F32)
    u = jnp.dot(xn, wu_ref[...], preferred_element_type=F32)
    a = (0.5 * (g * _sigmoid(g)) * u).astype(BF16)
    o_ref[...] += jnp.dot(a, wd_ref[...], preferred_element_type=F32)


def _ffn(h, gain, layer, weights, cast_next=None):
    wg, wu, wd = weights
    rows, d = h.shape
    ff = wg.shape[-1]
    tm = _pick(rows, (1056, 768, 512, 384, 256, 128))
    tf = _pick(ff, (512, 256, 128))
    nf = ff // tf
    cast = None
    if cast_next is not None:
        cast = _SideCast(cast_next[0], cast_next[1], (rows // tm) * nf, lambda i, f: i * nf + f)
        if not cast.ok:
            cast = None
    n_cast = cast.n if cast else 0
    outs = pl.pallas_call(
        functools.partial(_ffn_kernel, n_cast),
        grid=(rows // tm, nf),
        in_specs=[
            pl.BlockSpec((tm, d), lambda i, f: (i, 0)),
            pl.BlockSpec((None, 1, d), lambda i, f: (layer, 0, 0)),
            pl.BlockSpec((d, tf), lambda i, f: (0, f)),
            pl.BlockSpec((d, tf), lambda i, f: (0, f)),
            pl.BlockSpec((tf, d), lambda i, f: (f, 0)),
        ] + (cast.in_specs if cast else []),
        out_specs=[pl.BlockSpec((tm, d), lambda i, f: (i, 0))] + (cast.out_specs if cast else []),
        out_shape=[jax.ShapeDtypeStruct((rows, d), F32)] + (cast.out_shapes if cast else []),
        scratch_shapes=[pltpu.VMEM((tm, d), BF16)],
        compiler_params=_params(("arbitrary", "arbitrary")),
        name="ffn",
    )(h, gain, wg, wu, wd, *(cast.views if cast else []))
    if cast:
        return outs[0], cast.finish(outs[1:])
    fallback = None if cast_next is None else [w[cast_next[1]].astype(BF16) for w in cast_next[0]]
    return outs[0], fallback


def _norm_kernel(h_ref, gain_ref, o_ref):
    o_ref[...] = _rms_norm_bf16(h_ref[...], gain_ref[...])


def _norm(h, gain, layer):
    rows, d = h.shape
    tm = _pick(rows, (1056, 768, 384, 128))
    return pl.pallas_call(
        _norm_kernel,
        grid=(rows // tm,),
        in_specs=[pl.BlockSpec((tm, d), lambda i: (i, 0)),
                  pl.BlockSpec((None, 1, d), lambda i: (layer, 0, 0))],
        out_specs=pl.BlockSpec((tm, d), lambda i: (i, 0)),
        out_shape=jax.ShapeDtypeStruct((rows, d), BF16),
        compiler_params=_params(("arbitrary",)),
        name="mix_norm",
    )(h, gain)


def _proj_kernel(kind, xn_ref, w_ref, *rest):
    o_ref = rest[-1]
    tm, tn = o_ref.shape
    chunks = {"rope": 4, "gate": 2, "plain": 1}[kind]
    cm = tm // chunks
    for r in range(chunks):
        rows = slice(r * cm, (r + 1) * cm)
        z = jnp.dot(xn_ref[rows, :], w_ref[...], preferred_element_type=F32)
        if kind == "rope":
            tab_ref = rest[0]
            reps = tn // LANES
            cos = jnp.concatenate([tab_ref[0, rows, :]] * reps, axis=1)
            sin_lo = jnp.concatenate([tab_ref[1, rows, :]] * reps, axis=1)
            sin_hi = jnp.concatenate([tab_ref[2, rows, :]] * reps, axis=1)
            half = ROT_DIM // 2
            z = z * cos + pltpu.roll(z, half, 1) * sin_lo + pltpu.roll(z, tn - half, 1) * sin_hi
        elif kind == "gate":
            z = _sigmoid(z + rest[0][...])
        o_ref[rows, :] = z.astype(o_ref.dtype)


def _proj(xn, w, layer, col0, ncols, out_dtype, kind, extra=None, seq_len=None):
    rows, d = xn.shape
    tm = _pick(seq_len, (2112, 1408, 768, 384, 128)) if seq_len else _pick(rows, (2112, 1408, 768, 384, 128))
    tn = _pick(ncols // 2 if kind == "rope" else ncols, (512, 256, 128))
    assert col0 % tn == 0
    cb0 = col0 // tn
    in_specs = [
        pl.BlockSpec((tm, d), lambda i, n: (i, 0)),
        pl.BlockSpec((d, tn), lambda i, n: (0, cb0 + n)),
    ]
    args = [xn, w]
    if kind == "rope":
        tiles_per_seq = seq_len // tm
        q_blocks = ncols // 2 // tn
        in_specs.append(pl.BlockSpec((None, 3, tm, LANES),
                                     lambda i, n: (n // q_blocks, 0, i % tiles_per_seq, 0)))
        args.append(extra)
    elif kind == "gate":
        in_specs.append(pl.BlockSpec((None, 1, tn), lambda i, n: (layer, 0, n)))
        args.append(extra)
    return pl.pallas_call(
        functools.partial(_proj_kernel, kind),
        grid=(rows // tm, ncols // tn),
        in_specs=in_specs,
        out_specs=pl.BlockSpec((tm, tn), lambda i, n: (i, n)),
        out_shape=jax.ShapeDtypeStruct((rows, ncols), out_dtype),
        compiler_params=_params(("arbitrary", "arbitrary")),
        name="proj_" + kind,
    )(*args)


def _rope_tables(seq_len):
    half = ROT_DIM // 2
    inv_freq = ROPE_THETA ** (-jnp.arange(0, ROT_DIM, 2, dtype=F32) / ROT_DIM)
    ang = jnp.arange(seq_len, dtype=F32)[:, None] * inv_freq[None, :]
    cos, sin = jnp.cos(ang), jnp.sin(ang)
    ones = jnp.ones((seq_len, LANES - ROT_DIM), F32)
    zeros = jnp.zeros((seq_len, LANES - ROT_DIM), F32)
    zh = jnp.zeros((seq_len, half), F32)
    c = jnp.concatenate([cos, cos, ones], axis=1)
    s_lo = jnp.concatenate([zh, sin, zeros], axis=1)
    s_hi = jnp.concatenate([-sin, zh, zeros], axis=1)
    k_tab = jnp.stack([c, s_lo, s_hi])
    return jnp.stack([k_tab * ((HEAD_DIM ** -0.5) * math.log2(math.e)), k_tab])


def _attn_stages(k_ref, v_ref, qs, m_ref, l_ref, acc_ref, p_refs, a_refs):
    tq = qs[0].shape[0]
    rows_per_dot = min(tq, ATTN_ROWS_PER_DOT)

    def score_stage(start, width, masked, slot):
        kb = k_ref[pl.ds(start, width), :]
        for c in range(2):
            kc = kb[:, c * HEAD_DIM:(c + 1) * HEAD_DIM]
            for r0 in range(0, tq, rows_per_dot):
                rows = slice(r0, r0 + rows_per_dot)
                kw = min(width, r0 + rows_per_dot) if masked else width
                s = lax.dot_general(qs[c][rows], kc[0:kw], (((1,), (1,)), ((), ())),
                                    preferred_element_type=F32)
                if masked:
                    row = lax.broadcasted_iota(jnp.int32, s.shape, 0) + r0
                    col = lax.broadcasted_iota(jnp.int32, s.shape, 1)
                    s = jnp.where(col <= row, s, NEG_BIG)
                m_old = m_ref[c, rows, :]
                m_new = jnp.maximum(m_old, jnp.max(s, axis=-1, keepdims=True))
                alpha = jnp.exp2(m_old - m_new)
                p = jnp.exp2(s - jnp.tile(m_new, (1, kw // LANES)))
                l_ref[c, rows, :] = alpha * l_ref[c, rows, :] + jnp.sum(p, axis=-1, keepdims=True)
                m_ref[c, rows, :] = m_new
                a_refs[slot][c, rows, :] = alpha
                p_refs[slot][c, rows, 0:kw] = p.astype(BF16)
                if kw < width:
                    p_refs[slot][c, rows, kw:width] = jnp.zeros((rows_per_dot, width - kw), BF16)

    def value_stage(start, width, slot):
        vb = v_ref[pl.ds(start, width), :]
        for c in range(2):
            for r0 in range(0, tq, rows_per_dot):
                rows = slice(r0, r0 + rows_per_dot)
                pv = jnp.dot(p_refs[slot][c, rows, 0:width], vb, preferred_element_type=F32)
                acc_ref[c, rows, :] = (jnp.tile(a_refs[slot][c, rows, :], (1, V_DIM // LANES))
                                       * acc_ref[c, rows, :] + pv)

    return score_stage, value_stage


def _attn_init(m_ref, l_ref, acc_ref):
    m_ref[...] = jnp.full(m_ref.shape, NEG_BIG, F32)
    l_ref[...] = jnp.zeros(l_ref.shape, F32)
    acc_ref[...] = jnp.zeros(acc_ref.shape, F32)


def _attn_finish(lam_init, lq1_ref, lk1_ref, lq2_ref, lk2_ref, subln_ref, o_ref, l_ref, acc_ref):
    lam = (jnp.exp(jnp.sum(lq1_ref[...] * lk1_ref[...], keepdims=True))
           - jnp.exp(jnp.sum(lq2_ref[...] * lk2_ref[...], keepdims=True)) + lam_init)
    reps = V_DIM // LANES
    o = (acc_ref[0] * jnp.tile(1.0 / l_ref[0], (1, reps))
         - lam * (acc_ref[1] * jnp.tile(1.0 / l_ref[1], (1, reps))))
    o = o * lax.rsqrt(jnp.mean(o * o, axis=-1, keepdims=True) + EPS)
    o = o * subln_ref[...] * (1.0 - lam_init)
    o_ref[...] = o.astype(o_ref.dtype)


def _attn_main_kernel(lam_init, n_cast, q_ref, k_ref, v_ref, lq1_ref, lk1_ref, lq2_ref, lk2_ref,
                      subln_ref, *rest):
    cast_in, o_ref, cast_out = rest[:n_cast], rest[n_cast], rest[n_cast + 1:2 * n_cast + 1]
    m_ref, l_ref, acc_ref, p0_ref, p1_ref, a0_ref, a1_ref = rest[2 * n_cast + 1:]
    _SideCast.run(cast_in, cast_out)
    i = pl.program_id(2)
    tq = q_ref.shape[0]
    q = q_ref[...]
    score_stage, value_stage = _attn_stages(k_ref, v_ref, (q[:, :HEAD_DIM], q[:, HEAD_DIM:]),
                                            m_ref, l_ref, acc_ref, (p0_ref, p1_ref), (a0_ref, a1_ref))
    _attn_init(m_ref, l_ref, acc_ref)
    score_stage(pl.multiple_of(i * tq, tq), tq, True, 0)

    def chunk_start(n):
        return pl.multiple_of(jnp.where(n == 0, i, n - 1) * tq, tq)

    def body(n, carry):
        for slot in range(2):
            @pl.when(n % 2 == slot)
            def _():
                score_stage(chunk_start(n), tq, False, slot)
                value_stage(chunk_start(n - 1), tq, 1 - slot)
        return carry

    lax.fori_loop(1, i + 1, body, 0)
    for slot in range(2):
        @pl.when(i % 2 == slot)
        def _():
            value_stage(chunk_start(i), tq, slot)
    _attn_finish(lam_init, lq1_ref, lk1_ref, lq2_ref, lk2_ref, subln_ref, o_ref, l_ref, acc_ref)


def _attn_tail_kernel(lam_init, tk, q_ref, k_ref, v_ref, lq1_ref, lk1_ref, lq2_ref, lk2_ref, subln_ref,
                      o_in_ref, o_ref, m_ref, l_ref, acc_ref, p0_ref, p1_ref, a0_ref, a1_ref):
    del o_in_ref
    tq = q_ref.shape[0]
    n_full = (k_ref.shape[0] - tq) // tk
    q = q_ref[...]
    score_stage, value_stage = _attn_stages(k_ref, v_ref, (q[:, :HEAD_DIM], q[:, HEAD_DIM:]),
                                            m_ref, l_ref, acc_ref, (p0_ref, p1_ref), (a0_ref, a1_ref))
    _attn_init(m_ref, l_ref, acc_ref)
    chunks = [(n_full * tk, tq)] + [(j * tk, tk) for j in range(n_full)]
    score_stage(chunks[0][0], chunks[0][1], True, 0)
    for n in range(1, len(chunks)):
        score_stage(chunks[n][0], chunks[n][1], False, n % 2)
        value_stage(chunks[n - 1][0], chunks[n - 1][1], (n - 1) % 2)
    value_stage(chunks[-1][0], chunks[-1][1], (len(chunks) - 1) % 2)
    _attn_finish(lam_init, lq1_ref, lk1_ref, lq2_ref, lk2_ref, subln_ref, o_ref, l_ref, acc_ref)


def _attn_scratch(tq, tk):
    return [
        pltpu.VMEM((2, tq, LANES), F32),
        pltpu.VMEM((2, tq, LANES), F32),
        pltpu.VMEM((2, tq, V_DIM), F32),
        pltpu.VMEM((2, tq, tk), BF16),
        pltpu.VMEM((2, tq, tk), BF16),
        pltpu.VMEM((2, tq, LANES), F32),
        pltpu.VMEM((2, tq, LANES), F32),
    ]


def _attention(qk, v, lq1, lk1, lq2, lk2, subln, layer, lam_init, batch, seq_len, cast_next=None):
    n_heads = v.shape[1] // V_DIM
    tk = ATTN_CHUNK
    nq = seq_len // tk
    rem = seq_len - nq * tk
    qk3 = qk.reshape(batch, seq_len, qk.shape[1])
    v3 = v.reshape(batch, seq_len, v.shape[1])
    out_shape = jax.ShapeDtypeStruct((batch, seq_len, n_heads * V_DIM), BF16)

    def kv_param_specs(grid_rank):
        def at(*idx):
            return (lambda b, h, i: idx_of(b, h, idx)) if grid_rank == 3 else (lambda b, h: idx_of(b, h, idx))

        def idx_of(b, h, idx):
            return tuple(f(b, h) if callable(f) else f for f in idx)

        lam_spec = pl.BlockSpec((None, 1, HEAD_DIM), at(layer, 0, 0))
        return [pl.BlockSpec((None, seq_len, V_DIM), at(lambda b, h: b, 0, lambda b, h: n_heads + h)),
                pl.BlockSpec((None, seq_len, V_DIM), at(lambda b, h: b, 0, lambda b, h: h)),
                lam_spec, lam_spec, lam_spec, lam_spec,
                pl.BlockSpec((None, 1, V_DIM), at(layer, 0, 0))]

    cast = None
    if cast_next is not None:
        cast = _SideCast(cast_next[0], cast_next[1], batch * n_heads * nq,
                         lambda b, h, i: (b * n_heads + h) * nq + i)
        if not cast.ok:
            cast = None
    outs = pl.pallas_call(
        functools.partial(_attn_main_kernel, lam_init, cast.n if cast else 0),
        grid=(batch, n_heads, nq),
        in_specs=([pl.BlockSpec((None, tk, V_DIM), lambda b, h, i: (b, i, h))] + kv_param_specs(3)
                  + (cast.in_specs if cast else [])),
        out_specs=[pl.BlockSpec((None, tk, V_DIM), lambda b, h, i: (b, i, h))] + (cast.out_specs if cast else []),
        out_shape=[out_shape] + (cast.out_shapes if cast else []),
        scratch_shapes=_attn_scratch(tk, tk),
        compiler_params=_params(("arbitrary", "arbitrary", "arbitrary")),
        name="attn_main",
    )(qk3, qk3, v3, lq1, lk1, lq2, lk2, subln, *(cast.views if cast else []))
    out = outs[0]
    if cast:
        cast_out = cast.finish(outs[1:])
    else:
        cast_out = None if cast_next is None else [w[cast_next[1]].astype(BF16) for w in cast_next[0]]
    if rem:
        assert seq_len % rem == 0
        last = seq_len // rem - 1
        out = pl.pallas_call(
            functools.partial(_attn_tail_kernel, lam_init, tk),
            grid=(batch, n_heads),
            in_specs=([pl.BlockSpec((None, rem, V_DIM), lambda b, h: (b, last, h))] + kv_param_specs(2)
                      + [pl.BlockSpec(memory_space=pl.ANY)]),
            out_specs=pl.BlockSpec((None, rem, V_DIM), lambda b, h: (b, last, h)),
            out_shape=out_shape,
            scratch_shapes=_attn_scratch(rem, tk),
            input_output_aliases={8: 0},
            compiler_params=_params(("arbitrary", "arbitrary")),
            name="attn_tail",
        )(qk3, qk3, v3, lq1, lk1, lq2, lk2, subln, out)
    return out.reshape(batch * seq_len, n_heads * V_DIM), cast_out


def _lru_kernel(xr_ref, yr_ref, p_ref, w_ref, o_ref, halo_ref, carry_ref, a_ref, b_ref, h_ref):
    t = pl.program_id(2)
    tt, cw = xr_ref.shape
    halo = SUBLANES

    @pl.when(t == 0)
    def _():
        halo_ref[...] = jnp.zeros(halo_ref.shape, F32)
        carry_ref[...] = jnp.zeros(carry_ref.shape, F32)

    prm = p_ref[...]
    conv_b, ga_b, gx_b = prm[0:1], prm[1:2], prm[2:3]
    half_scale = (0.5 * LRU_C) * jax.nn.log_sigmoid(prm[3:4])
    row8 = lax.broadcasted_iota(jnp.int32, (halo, cw), 0)
    nl = cw // LANES
    rc = tt // LRU_ROW_CHUNKS
    ng = tt // SUBLANES
    assert ng % SUBLANES == 0 and rc % SUBLANES == 0

    def scan_rows(t0):
        s, g0 = divmod(t0, ng)
        return pl.ds(g0 * SUBLANES + s, SUBLANES, stride=SUBLANES)

    for c0 in range(0, tt, rc):
        x = xr_ref[c0:c0 + rc, :]
        prev = halo_ref[...] if c0 == 0 else xr_ref[c0 - halo:c0, :]
        xc = conv_b + x * prm[4:5]
        for j in range(1, CONV_W):
            xs = pltpu.roll(x, j, 0)
            head = jnp.where(row8 < j, pltpu.roll(prev, j, 0), xs[0:halo, :])
            xc = xc + jnp.concatenate([head, xs[halo:, :]], axis=0) * prm[4 + j:5 + j]

        gz = jnp.dot(xc.astype(BF16), w_ref[...], preferred_element_type=F32)
        tr = jnp.tanh(gz[:, :cw] + ga_b)
        ti = jnp.tanh(gz[:, cw:] + gx_b)
        log_a = tr * half_scale + half_scale
        a = jnp.exp(log_a)
        y = -jnp.tanh(log_a) * (a * a + 1.0)
        mult = jnp.where(y > 0.0, y * lax.rsqrt(y), 0.0)
        xh = 0.5 * xc
        bx = mult * (ti * xh + xh)
        for r0 in range(0, rc, SUBLANES):
            dst = scan_rows(c0 + r0)
            for k in range(nl):
                a_ref[k, dst, :] = a[r0:r0 + SUBLANES, k * LANES:(k + 1) * LANES]
                b_ref[k, dst, :] = bx[r0:r0 + SUBLANES, k * LANES:(k + 1) * LANES]
    halo_ref[...] = xr_ref[tt - halo:tt, :]

    def group(ref, g):
        rows = pl.ds(pl.multiple_of(g * SUBLANES, SUBLANES), SUBLANES)
        return jnp.concatenate([ref[k, rows, :] for k in range(nl)], axis=1)

    def put_group(ref, g, val):
        rows = pl.ds(pl.multiple_of(g * SUBLANES, SUBLANES), SUBLANES)
        for k in range(nl):
            ref[k, rows, :] = val[:, k * LANES:(k + 1) * LANES]

    def block_scan(g, state):
        h, cum = state
        ag = group(a_ref, g)
        h = ag * h + group(b_ref, g)
        cum = ag * cum
        put_group(h_ref, g, h)
        put_group(a_ref, g, cum)
        return h, cum

    h, cum = lax.fori_loop(0, ng, block_scan,
                           (jnp.zeros((SUBLANES, cw), F32), jnp.ones((SUBLANES, cw), F32)),
                           unroll=LRU_SCAN_UNROLL)
    e = carry_ref[...]
    entry = []
    for s in range(SUBLANES):
        entry.append(e)
        e = cum[s:s + 1, :] * e + h[s:s + 1, :]
    carry_ref[...] = e
    entry = jnp.concatenate(entry, axis=0)

    def block_fix(g, carry):
        put_group(h_ref, g, group(h_ref, g) + group(a_ref, g) * entry)
        return carry

    lax.fori_loop(0, ng, block_fix, 0, unroll=LRU_SCAN_UNROLL)

    for c0 in range(0, tt, rc):
        hs = jnp.concatenate(
            [jnp.concatenate([h_ref[k, scan_rows(c0 + r0), :] for k in range(nl)], axis=1)
             for r0 in range(0, rc, SUBLANES)], axis=0)
        o_ref[c0:c0 + rc, :] = (hs * jax.nn.gelu(yr_ref[c0:c0 + rc, :])).astype(o_ref.dtype)


def _lru(z_lru, prm, w_gate, layer, batch, seq_len):
    rows = z_lru.shape[0]
    width = z_lru.shape[1] // 2
    cw = LRU_SUPER
    ns = width // cw
    tt = SEQ_BLOCK
    nt = seq_len // tt
    return pl.pallas_call(
        _lru_kernel,
        grid=(batch, ns, nt),
        in_specs=[
            pl.BlockSpec((tt, cw), lambda b, s, t: (b * nt + t, s)),
            pl.BlockSpec((tt, cw), lambda b, s, t: (b * nt + t, ns + s)),
            pl.BlockSpec((None, 2 * CONV_W, cw), lambda b, s, t: (layer, 0, s)),
            pl.BlockSpec((None, None, cw, 2 * cw), lambda b, s, t: (layer, s, 0, 0)),
        ],
        out_specs=pl.BlockSpec((tt, cw), lambda b, s, t: (b * nt + t, s)),
        out_shape=jax.ShapeDtypeStruct((rows, width), BF16),
        scratch_shapes=[
            pltpu.VMEM((SUBLANES, cw), F32),
            pltpu.VMEM((1, cw), F32),
            pltpu.VMEM((cw // LANES, tt, LANES), F32),
            pltpu.VMEM((cw // LANES, tt, LANES), F32),
            pltpu.VMEM((cw // LANES, tt, LANES), F32),
        ],
        compiler_params=_params(("arbitrary", "arbitrary", "arbitrary")),
        name="rglru",
    )(z_lru, z_lru, prm, w_gate)


def _lru_gate_weights(ga_w, gx_w):
    depth, nb, bw, _ = ga_w.shape
    per = LRU_SUPER // bw
    ns = nb // per

    def dense(w):
        w = w.reshape(depth, ns, per, bw, bw)
        eye = jnp.eye(per, dtype=w.dtype)
        full = jnp.einsum("lspkj,pq->lspkqj", w, eye)
        return full.reshape(depth, ns, per * bw, per * bw)

    return (0.5 * jnp.concatenate([dense(ga_w), dense(gx_w)], axis=-1)).astype(BF16)


def _mix_kernel(oa_ref, ol_ref, wa_ref, wl_ref, ga_ref, gl_ref, o_ref):
    ya = jnp.dot(oa_ref[...], wa_ref[...], preferred_element_type=F32)
    yl = jnp.dot(ol_ref[...], wl_ref[...], preferred_element_type=F32)
    o_ref[...] = (ga_ref[...] * ya + gl_ref[...] * yl).astype(o_ref.dtype)


def _mix(oa, ol, w_ba, w_bl, gates, layer):
    rows, da = oa.shape
    dl = ol.shape[1]
    d = w_ba.shape[-1]
    tm = _pick(rows, (1408, 1056, 768, 384, 128))
    tn = _pick(d, (512, 256, 128))
    nb = d // tn
    return pl.pallas_call(
        _mix_kernel,
        grid=(rows // tm, nb),
        in_specs=[
            pl.BlockSpec((tm, da), lambda i, n: (i, 0)),
            pl.BlockSpec((tm, dl), lambda i, n: (i, 0)),
            pl.BlockSpec((None, da, tn), lambda i, n: (layer, 0, n)),
            pl.BlockSpec((None, dl, tn), lambda i, n: (layer, 0, n)),
            pl.BlockSpec((tm, tn), lambda i, n: (i, n)),
            pl.BlockSpec((tm, tn), lambda i, n: (i, nb + n)),
        ],
        out_specs=pl.BlockSpec((tm, tn), lambda i, n: (i, n)),
        out_shape=jax.ShapeDtypeStruct((rows, d), BF16),
        compiler_params=_params(("arbitrary", "arbitrary")),
        name="branch_mix",
    )(oa, ol, w_ba, w_bl, gates, gates)


def _out_kernel(y_ref, w_ref, h_ref, o_ref):
    o_ref[...] = h_ref[...] + jnp.dot(y_ref[...], w_ref[...], preferred_element_type=F32)


def _out_proj(y, w_o, h, layer):
    rows, d = h.shape
    tm = _pick(rows, (2112, 1056, 768, 384, 128))
    tn = _pick(d, (512, 256, 128))
    return pl.pallas_call(
        _out_kernel,
        grid=(rows // tm, d // tn),
        in_specs=[
            pl.BlockSpec((tm, y.shape[1]), lambda i, n: (i, 0)),
            pl.BlockSpec((None, y.shape[1], tn), lambda i, n: (layer, 0, n)),
            pl.BlockSpec((tm, tn), lambda i, n: (i, n)),
        ],
        out_specs=pl.BlockSpec((tm, tn), lambda i, n: (i, n)),
        out_shape=jax.ShapeDtypeStruct((rows, d), F32),
        compiler_params=_params(("arbitrary", "arbitrary")),
        name="out_proj",
    )(y, w_o, h)


def _final_norm_kernel(h_ref, next_ref, g_ref, o_ref):
    x = jnp.concatenate([h_ref[N_META:, :], next_ref[...]], axis=0)
    ms = jnp.mean(x * x, axis=-1, keepdims=True)
    o_ref[...] = x * lax.rsqrt(ms + EPS) * g_ref[...]


def _final_norm(h, gain, batch, seq_len, seq):
    d = h.shape[1]
    tb = _pick(seq, (512, 256, 128))
    assert tb % N_META == 0 and seq + N_META <= seq_len
    h3 = h.reshape(batch, seq_len, d)
    return pl.pallas_call(
        _final_norm_kernel,
        grid=(batch, seq // tb),
        in_specs=[pl.BlockSpec((None, tb, d), lambda b, j: (b, j, 0)),
                  pl.BlockSpec((None, N_META, d), lambda b, j: (b, (j + 1) * (tb // N_META), 0)),
                  pl.BlockSpec((1, d), lambda b, j: (0, 0))],
        out_specs=pl.BlockSpec((None, tb, d), lambda b, j: (b, j, 0)),
        out_shape=jax.ShapeDtypeStruct((batch, seq, d), F32),
        compiler_params=_params(("arbitrary", "arbitrary")),
        name="final_norm",
    )(h3, h3, gain)


def kernel(x, meta_tokens, norm_ffn1, ffn1_w_gate, ffn1_w_up, ffn1_w_down, norm_mix, w_in, b_gate, lambda_q1, lambda_k1, lambda_q2, lambda_k2, attn_subln, conv_w, conv_b, gate_x_w, gate_x_b, gate_a_w, gate_a_b, lru_a_param, w_branch_attn, w_branch_lru, w_out, norm_ffn2, ffn2_w_gate, ffn2_w_up, ffn2_w_down, final_norm):
    batch, seq, d = x.shape
    depth = w_in.shape[0]
    attn_v = w_branch_attn.shape[1]
    attn_qk = attn_v
    lru_w = w_branch_lru.shape[1]
    seq_real = N_META + seq
    tp = -(-seq_real // SEQ_BLOCK) * SEQ_BLOCK
    rows = batch * tp

    meta = jnp.broadcast_to(meta_tokens.astype(x.dtype)[None], (batch, N_META, d))
    h = jnp.concatenate([meta, x, jnp.zeros((batch, tp - seq_real, d), x.dtype)], axis=1)
    h = h.reshape(rows, d)

    def vec(p):
        return p[:, None, :]

    bf = lambda w: w.astype(BF16)
    ffn1_f32 = (ffn1_w_gate, ffn1_w_up, ffn1_w_down)
    ffn2_f32 = (ffn2_w_gate, ffn2_w_up, ffn2_w_down)
    f1 = [bf(w[0]) for w in ffn1_f32]
    f2 = [bf(w[0]) for w in ffn2_f32]
    w_in_l = bf(w_in[0])
    w_ba, w_bl, w_o = bf(w_branch_attn), bf(w_branch_lru), bf(w_out)
    w_lru_gate = _lru_gate_weights(gate_a_w, gate_x_w)
    lru_prm = jnp.concatenate([conv_b[:, None], 0.5 * gate_a_b[:, None], 0.5 * gate_x_b[:, None],
                               lru_a_param[:, None], conv_w], axis=1)
    rope = _rope_tables(tp)
    n1, nm, n2 = vec(norm_ffn1), vec(norm_mix), vec(norm_ffn2)
    bg = vec(b_gate)
    lq1, lk1, lq2, lk2 = vec(lambda_q1), vec(lambda_k1), vec(lambda_q2), vec(lambda_k2)
    subln = vec(attn_subln)

    c_v = 2 * attn_qk
    c_lru = c_v + attn_v
    c_gate = c_lru + 2 * lru_w
    for l in range(depth):
        lam_init = 0.8 - 0.6 * math.exp(-0.3 * l)
        more = l + 1 < depth
        h, f1 = _ffn(h, n1, l, f1, (ffn1_f32, l + 1) if more else None)
        xn = _norm(h, nm, l)
        qk = _proj(xn, w_in_l, l, 0, 2 * attn_qk, BF16, "rope", rope, seq_len=tp)
        v = _proj(xn, w_in_l, l, c_v, attn_v, BF16, "plain")
        z_lru = _proj(xn, w_in_l, l, c_lru, 2 * lru_w, F32, "plain")
        gates = _proj(xn, w_in_l, l, c_gate, 2 * d, F32, "gate", bg)
        o_attn, w_in_next = _attention(qk, v, lq1, lk1, lq2, lk2, subln, l, lam_init, batch, tp,
                                       ((w_in,), l + 1) if more else None)
        if more:
            w_in_l = w_in_next[0]
        o_lru = _lru(z_lru, lru_prm, w_lru_gate, l, batch, tp)
        y = _mix(o_attn, o_lru, w_ba, w_bl, gates, l)
        h = _out_proj(y, w_o, h, l)
        h, f2 = _ffn(h, n2, l, f2, (ffn2_f32, l + 1) if more else None)
    return _final_norm(h, final_norm[None, :], batch, tp, seq)
```

```python
import functools
import math

import jax
import jax.numpy as jnp
from jax import lax
from jax.experimental import pallas as pl
from jax.experimental.pallas import tpu as pltpu

N_META = 16
HEAD_DIM = 128
V_DIM = 2 * HEAD_DIM
ROT_DIM = HEAD_DIM // 4
ROPE_THETA = 500000.0
CONV_W = 4
LRU_C = 8.0
EPS = 1e-6

LANES = 128
SUBLANES = 8
BF16_ROWS = 16
SEQ_BLOCK = 384
ATTN_CHUNK = 2 * SEQ_BLOCK
ATTN_ROWS_PER_DOT = SEQ_BLOCK
LRU_SUPER = 640
LRU_TILE_ROWS = (1408, 384)
LRU_CHUNK_ROWS = 352
LRU_SCAN_UNROLL = 8
VMEM_LIMIT_BYTES = 60 * 1024 * 1024
NEG_BIG = -1e30

BF16 = jnp.bfloat16
F32 = jnp.float32


def _pick(n, prefs):
    for p in prefs:
        if p <= n and n % p == 0:
            return p
    return n


def _params(semantics):
    return pltpu.CompilerParams(dimension_semantics=semantics, vmem_limit_bytes=VMEM_LIMIT_BYTES)


def _sigmoid(x):
    return 0.5 * jnp.tanh(0.5 * x) + 0.5


def _rms_norm_bf16(x, gain):
    ms = jnp.mean(x * x, axis=-1, keepdims=True)
    return (x * lax.rsqrt(ms + EPS) * gain).astype(BF16)


class _SideCast:
    def __init__(self, stacked, layer, n_steps, step_of):
        self.views = list(stacked)
        self.n = len(stacked)
        self.in_specs, self.out_specs, self.out_shapes = [], [], []
        self.ok = True
        for w in stacked:
            _, rows, cols = w.shape
            fits = [rb for rb in range(BF16_ROWS, rows + 1, BF16_ROWS)
                    if rows % rb == 0 and rb * n_steps >= rows]
            if not fits:
                self.ok = False
                continue
            rb = fits[0]
            last = rows // rb - 1
            self.in_specs.append(pl.BlockSpec(
                (None, rb, cols), lambda *g, last=last: (layer, jnp.minimum(step_of(*g), last), 0)))
            self.out_specs.append(pl.BlockSpec(
                (rb, cols), lambda *g, last=last: (jnp.minimum(step_of(*g), last), 0)))
            self.out_shapes.append(jax.ShapeDtypeStruct((rows, cols), BF16))

    @staticmethod
    def run(in_refs, out_refs):
        for src, dst in zip(in_refs, out_refs):
            dst[...] = src[...].astype(BF16)

    def finish(self, outs):
        return list(outs)


def _ffn_kernel(n_cast, h_ref, gain_ref, wg_ref, wu_ref, wd_ref, *rest):
    cast_in, o_ref = rest[:n_cast], rest[n_cast]
    cast_out, xn_ref = rest[n_cast + 1:2 * n_cast + 1], rest[-1]
    _SideCast.run(cast_in, cast_out)
    f = pl.program_id(1)

    @pl.when(f == 0)
    def _():
        x = h_ref[...]
        xn_ref[...] = _rms_norm_bf16(x, gain_ref[...])
        o_ref[...] = x

    xn = xn_ref[...]
    g = jnp.dot(xn, wg_ref[...], preferred_element_type=F32)
    u = jnp.dot(xn, wu_ref[...], preferred_element_type=F32)
    a = (0.5 * (g * _sigmoid(g)) * u).astype(BF16)
    o_ref[...] += jnp.dot(a, wd_ref[...], preferred_element_type=F32)


def _ffn(h, gain, layer, weights, cast_next=None):
    wg, wu, wd = weights
    rows, d = h.shape
    ff = wg.shape[-1]
    tm = _pick(rows, (1056, 768, 512, 384, 256, 128))
    tf = _pick(ff, (512, 256, 128))
    nf = ff // tf
    cast = None
    if cast_next is not None:
        cast = _SideCast(cast_next[0], cast_next[1], (rows // tm) * nf, lambda i, f: i * nf + f)
        if not cast.ok:
            cast = None
    n_cast = cast.n if cast else 0
    outs = pl.pallas_call(
        functools.partial(_ffn_kernel, n_cast),
        grid=(rows // tm, nf),
        in_specs=[
            pl.BlockSpec((tm, d), lambda i, f: (i, 0)),
            pl.BlockSpec((None, 1, d), lambda i, f: (layer, 0, 0)),
            pl.BlockSpec((d, tf), lambda i, f: (0, f)),
            pl.BlockSpec((d, tf), lambda i, f: (0, f)),
            pl.BlockSpec((tf, d), lambda i, f: (f, 0)),
        ] + (cast.in_specs if cast else []),
        out_specs=[pl.BlockSpec((tm, d), lambda i, f: (i, 0))] + (cast.out_specs if cast else []),
        out_shape=[jax.ShapeDtypeStruct((rows, d), F32)] + (cast.out_shapes if cast else []),
        scratch_shapes=[pltpu.VMEM((tm, d), BF16)],
        compiler_params=_params(("arbitrary", "arbitrary")),
        name="ffn",
    )(h, gain, wg, wu, wd, *(cast.views if cast else []))
    if cast:
        return outs[0], cast.finish(outs[1:])
    fallback = None if cast_next is None else [w[cast_next[1]].astype(BF16) for w in cast_next[0]]
    return outs[0], fallback


def _norm_kernel(h_ref, gain_ref, o_ref):
    o_ref[...] = _rms_norm_bf16(h_ref[...], gain_ref[...])


def _norm(h, gain, layer):
    rows, d = h.shape
    tm = _pick(rows, (1056, 768, 384, 128))
    return pl.pallas_call(
        _norm_kernel,
        grid=(rows // tm,),
        in_specs=[pl.BlockSpec((tm, d), lambda i: (i, 0)),
                  pl.BlockSpec((None, 1, d), lambda i: (layer, 0, 0))],
        out_specs=pl.BlockSpec((tm, d), lambda i: (i, 0)),
        out_shape=jax.ShapeDtypeStruct((rows, d), BF16),
        compiler_params=_params(("arbitrary",)),
        name="mix_norm",
    )(h, gain)


def _proj_kernel(kind, xn_ref, w_ref, *rest):
    o_ref = rest[-1]
    tm, tn = o_ref.shape
    chunks = {"rope": 6 if tm % (6 * BF16_ROWS) == 0 else 4, "gate": 2, "plain": 1}[kind]
    cm = tm // chunks
    for r in range(chunks):
        rows = slice(r * cm, (r + 1) * cm)
        z = jnp.dot(xn_ref[rows, :], w_ref[...], preferred_element_type=F32)
        if kind == "rope":
            tab_ref = rest[0]
            reps = tn // LANES
            cos = jnp.concatenate([tab_ref[0, rows, :]] * reps, axis=1)
            sin_lo = jnp.concatenate([tab_ref[1, rows, :]] * reps, axis=1)
            sin_hi = jnp.concatenate([tab_ref[2, rows, :]] * reps, axis=1)
            half = ROT_DIM // 2
            z = z * cos + pltpu.roll(z, half, 1) * sin_lo + pltpu.roll(z, tn - half, 1) * sin_hi
        elif kind == "gate":
            z = _sigmoid(z + rest[0][...])
        o_ref[rows, :] = z.astype(o_ref.dtype)


def _proj(xn, w, layer, col0, ncols, out_dtype, kind, extra=None, seq_len=None):
    rows, d = xn.shape
    tm = _pick(seq_len, (2112, 1408, 768, 384, 128)) if seq_len else _pick(rows, (2112, 1408, 768, 384, 128))
    tn = _pick(ncols // 2 if kind == "rope" else ncols, (512, 256, 128))
    assert col0 % tn == 0
    cb0 = col0 // tn
    in_specs = [
        pl.BlockSpec((tm, d), lambda i, n: (i, 0)),
        pl.BlockSpec((d, tn), lambda i, n: (0, cb0 + n)),
    ]
    args = [xn, w]
    if kind == "rope":
        tiles_per_seq = seq_len // tm
        q_blocks = ncols // 2 // tn
        in_specs.append(pl.BlockSpec((None, 3, tm, LANES),
                                     lambda i, n: (n // q_blocks, 0, i % tiles_per_seq, 0)))
        args.append(extra)
    elif kind == "gate":
        in_specs.append(pl.BlockSpec((None, 1, tn), lambda i, n: (layer, 0, n)))
        args.append(extra)
    return pl.pallas_call(
        functools.partial(_proj_kernel, kind),
        grid=(rows // tm, ncols // tn),
        in_specs=in_specs,
        out_specs=pl.BlockSpec((tm, tn), lambda i, n: (i, n)),
        out_shape=jax.ShapeDtypeStruct((rows, ncols), out_dtype),
        compiler_params=_params(("arbitrary", "arbitrary")),
        name="proj_" + kind,
    )(*args)


def _rope_tables(seq_len):
    half = ROT_DIM // 2
    inv_freq = ROPE_THETA ** (-jnp.arange(0, ROT_DIM, 2, dtype=F32) / ROT_DIM)
    ang = jnp.arange(seq_len, dtype=F32)[:, None] * inv_freq[None, :]
    cos, sin = jnp.cos(ang), jnp.sin(ang)
    ones = jnp.ones((seq_len, LANES - ROT_DIM), F32)
    zeros = jnp.zeros((seq_len, LANES - ROT_DIM), F32)
    zh = jnp.zeros((seq_len, half), F32)
    c = jnp.concatenate([cos, cos, ones], axis=1)
    s_lo = jnp.concatenate([zh, sin, zeros], axis=1)
    s_hi = jnp.concatenate([-sin, zh, zeros], axis=1)
    k_tab = jnp.stack([c, s_lo, s_hi])
    return jnp.stack([k_tab * ((HEAD_DIM ** -0.5) * math.log2(math.e)), k_tab])


def _attn_stages(k_ref, v_ref, qs, m_ref, l_ref, acc_ref, p_refs, a_refs):
    tq = qs[0].shape[0]
    rows_per_dot = min(tq, ATTN_ROWS_PER_DOT)

    def score_stage(start, width, masked, slot):
        kb = k_ref[pl.ds(start, width), :]
        for c in range(2):
            kc = kb[:, c * HEAD_DIM:(c + 1) * HEAD_DIM]
            for r0 in range(0, tq, rows_per_dot):
                rows = slice(r0, r0 + rows_per_dot)
                kw = min(width, r0 + rows_per_dot) if masked else width
                s = lax.dot_general(qs[c][rows], kc[0:kw], (((1,), (1,)), ((), ())),
                                    preferred_element_type=F32)
                if masked:
                    row = lax.broadcasted_iota(jnp.int32, s.shape, 0) + r0
                    col = lax.broadcasted_iota(jnp.int32, s.shape, 1)
                    s = jnp.where(col <= row, s, NEG_BIG)
                m_old = m_ref[c, rows, :]
                m_new = jnp.maximum(m_old, jnp.max(s, axis=-1, keepdims=True))
                alpha = jnp.exp2(m_old - m_new)
                p = jnp.exp2(s - jnp.tile(m_new, (1, kw // LANES)))
                l_ref[c, rows, :] = alpha * l_ref[c, rows, :] + jnp.sum(p, axis=-1, keepdims=True)
                m_ref[c, rows, :] = m_new
                a_refs[slot][c, rows, :] = alpha
                p_refs[slot][c, rows, 0:kw] = p.astype(BF16)
                if kw < width:
                    p_refs[slot][c, rows, kw:width] = jnp.zeros((rows_per_dot, width - kw), BF16)

    def value_stage(start, width, slot):
        vb = v_ref[pl.ds(start, width), :]
        for c in range(2):
            for r0 in range(0, tq, rows_per_dot):
                rows = slice(r0, r0 + rows_per_dot)
                pv = jnp.dot(p_refs[slot][c, rows, 0:width], vb, preferred_element_type=F32)
                acc_ref[c, rows, :] = (jnp.tile(a_refs[slot][c, rows, :], (1, V_DIM // LANES))
                                       * acc_ref[c, rows, :] + pv)

    return score_stage, value_stage


def _attn_init(m_ref, l_ref, acc_ref):
    m_ref[...] = jnp.full(m_ref.shape, NEG_BIG, F32)
    l_ref[...] = jnp.zeros(l_ref.shape, F32)
    acc_ref[...] = jnp.zeros(acc_ref.shape, F32)


def _attn_finish(lam_init, lq1_ref, lk1_ref, lq2_ref, lk2_ref, subln_ref, o_ref, l_ref, acc_ref):
    lam = (jnp.exp(jnp.sum(lq1_ref[...] * lk1_ref[...], keepdims=True))
           - jnp.exp(jnp.sum(lq2_ref[...] * lk2_ref[...], keepdims=True)) + lam_init)
    reps = V_DIM // LANES
    o = (acc_ref[0] * jnp.tile(1.0 / l_ref[0], (1, reps))
         - lam * (acc_ref[1] * jnp.tile(1.0 / l_ref[1], (1, reps))))
    o = o * lax.rsqrt(jnp.mean(o * o, axis=-1, keepdims=True) + EPS)
    o = o * subln_ref[...] * (1.0 - lam_init)
    o_ref[...] = o.astype(o_ref.dtype)


def _attn_main_kernel(lam_init, n_cast, q_ref, k_ref, v_ref, lq1_ref, lk1_ref, lq2_ref, lk2_ref,
                      subln_ref, *rest):
    cast_in, o_ref, cast_out = rest[:n_cast], rest[n_cast], rest[n_cast + 1:2 * n_cast + 1]
    m_ref, l_ref, acc_ref, p0_ref, p1_ref, a0_ref, a1_ref = rest[2 * n_cast + 1:]
    _SideCast.run(cast_in, cast_out)
    i = pl.program_id(2)
    tq = q_ref.shape[0]
    q = q_ref[...]
    score_stage, value_stage = _attn_stages(k_ref, v_ref, (q[:, :HEAD_DIM], q[:, HEAD_DIM:]),
                                            m_ref, l_ref, acc_ref, (p0_ref, p1_ref), (a0_ref, a1_ref))
    _attn_init(m_ref, l_ref, acc_ref)
    score_stage(pl.multiple_of(i * tq, tq), tq, True, 0)

    def chunk_start(n):
        return pl.multiple_of(jnp.where(n == 0, i, n - 1) * tq, tq)

    def body(n, carry):
        for slot in range(2):
            @pl.when(n % 2 == slot)
            def _():
                score_stage(chunk_start(n), tq, False, slot)
                value_stage(chunk_start(n - 1), tq, 1 - slot)
        return carry

    lax.fori_loop(1, i + 1, body, 0)
    for slot in range(2):
        @pl.when(i % 2 == slot)
        def _():
            value_stage(chunk_start(i), tq, slot)
    _attn_finish(lam_init, lq1_ref, lk1_ref, lq2_ref, lk2_ref, subln_ref, o_ref, l_ref, acc_ref)


def _attn_tail_kernel(lam_init, tk, q_ref, k_ref, v_ref, lq1_ref, lk1_ref, lq2_ref, lk2_ref, subln_ref,
                      o_in_ref, o_ref, m_ref, l_ref, acc_ref, p0_ref, p1_ref, a0_ref, a1_ref):
    del o_in_ref
    tq = q_ref.shape[0]
    n_full = (k_ref.shape[0] - tq) // tk
    q = q_ref[...]
    score_stage, value_stage = _attn_stages(k_ref, v_ref, (q[:, :HEAD_DIM], q[:, HEAD_DIM:]),
                                            m_ref, l_ref, acc_ref, (p0_ref, p1_ref), (a0_ref, a1_ref))
    _attn_init(m_ref, l_ref, acc_ref)
    chunks = [(n_full * tk, tq)] + [(j * tk, tk) for j in range(n_full)]
    score_stage(chunks[0][0], chunks[0][1], True, 0)
    for n in range(1, len(chunks)):
        score_stage(chunks[n][0], chunks[n][1], False, n % 2)
        value_stage(chunks[n - 1][0], chunks[n - 1][1], (n - 1) % 2)
    value_stage(chunks[-1][0], chunks[-1][1], (len(chunks) - 1) % 2)
    _attn_finish(lam_init, lq1_ref, lk1_ref, lq2_ref, lk2_ref, subln_ref, o_ref, l_ref, acc_ref)


def _attn_scratch(tq, tk):
    return [
        pltpu.VMEM((2, tq, LANES), F32),
        pltpu.VMEM((2, tq, LANES), F32),
        pltpu.VMEM((2, tq, V_DIM), F32),
        pltpu.VMEM((2, tq, tk), BF16),
        pltpu.VMEM((2, tq, tk), BF16),
        pltpu.VMEM((2, tq, LANES), F32),
        pltpu.VMEM((2, tq, LANES), F32),
    ]


def _attention(qk, v, lq1, lk1, lq2, lk2, subln, layer, lam_init, batch, seq_len, cast_next=None):
    n_heads = v.shape[1] // V_DIM
    tk = ATTN_CHUNK
    nq = seq_len // tk
    rem = seq_len - nq * tk
    qk3 = qk.reshape(batch, seq_len, qk.shape[1])
    v3 = v.reshape(batch, seq_len, v.shape[1])
    out_shape = jax.ShapeDtypeStruct((batch, seq_len, n_heads * V_DIM), BF16)

    def kv_param_specs(grid_rank):
        def at(*idx):
            return (lambda b, h, i: idx_of(b, h, idx)) if grid_rank == 3 else (lambda b, h: idx_of(b, h, idx))

        def idx_of(b, h, idx):
            return tuple(f(b, h) if callable(f) else f for f in idx)

        lam_spec = pl.BlockSpec((None, 1, HEAD_DIM), at(layer, 0, 0))
        return [pl.BlockSpec((None, seq_len, V_DIM), at(lambda b, h: b, 0, lambda b, h: n_heads + h)),
                pl.BlockSpec((None, seq_len, V_DIM), at(lambda b, h: b, 0, lambda b, h: h)),
                lam_spec, lam_spec, lam_spec, lam_spec,
                pl.BlockSpec((None, 1, V_DIM), at(layer, 0, 0))]

    cast = None
    if cast_next is not None:
        cast = _SideCast(cast_next[0], cast_next[1], batch * n_heads * nq,
                         lambda b, h, i: (b * n_heads + h) * nq + i)
        if not cast.ok:
            cast = None
    outs = pl.pallas_call(
        functools.partial(_attn_main_kernel, lam_init, cast.n if cast else 0),
        grid=(batch, n_heads, nq),
        in_specs=([pl.BlockSpec((None, tk, V_DIM), lambda b, h, i: (b, i, h))] + kv_param_specs(3)
                  + (cast.in_specs if cast else [])),
        out_specs=[pl.BlockSpec((None, tk, V_DIM), lambda b, h, i: (b, i, h))] + (cast.out_specs if cast else []),
        out_shape=[out_shape] + (cast.out_shapes if cast else []),
        scratch_shapes=_attn_scratch(tk, tk),
        compiler_params=_params(("arbitrary", "arbitrary", "arbitrary")),
        name="attn_main",
    )(qk3, qk3, v3, lq1, lk1, lq2, lk2, subln, *(cast.views if cast else []))
    out = outs[0]
    if cast:
        cast_out = cast.finish(outs[1:])
    else:
        cast_out = None if cast_next is None else [w[cast_next[1]].astype(BF16) for w in cast_next[0]]
    if rem:
        assert seq_len % rem == 0
        last = seq_len // rem - 1
        out = pl.pallas_call(
            functools.partial(_attn_tail_kernel, lam_init, tk),
            grid=(batch, n_heads),
            in_specs=([pl.BlockSpec((None, rem, V_DIM), lambda b, h: (b, last, h))] + kv_param_specs(2)
                      + [pl.BlockSpec(memory_space=pl.ANY)]),
            out_specs=pl.BlockSpec((None, rem, V_DIM), lambda b, h: (b, last, h)),
            out_shape=out_shape,
            scratch_shapes=_attn_scratch(rem, tk),
            input_output_aliases={8: 0},
            compiler_params=_params(("arbitrary", "arbitrary")),
            name="attn_tail",
        )(qk3, qk3, v3, lq1, lk1, lq2, lk2, subln, out)
    return out.reshape(batch * seq_len, n_heads * V_DIM), cast_out


def _lru_kernel(n_cast, xr_ref, yr_ref, p_ref, w_ref, *rest):
    cast_in, o_ref, cast_out = rest[:n_cast], rest[n_cast], rest[n_cast + 1:2 * n_cast + 1]
    halo_ref, carry_ref, a_ref, b_ref, h_ref = rest[2 * n_cast + 1:]
    _SideCast.run(cast_in, cast_out)
    t = pl.program_id(2)
    tt, cw = xr_ref.shape
    halo = SUBLANES

    @pl.when(t == 0)
    def _():
        halo_ref[...] = jnp.zeros(halo_ref.shape, F32)
        carry_ref[...] = jnp.zeros(carry_ref.shape, F32)

    prm = p_ref[...]
    conv_b, ga_b, gx_b = prm[0:1], prm[1:2], prm[2:3]
    half_scale = (0.5 * LRU_C) * jax.nn.log_sigmoid(prm[3:4])
    row8 = lax.broadcasted_iota(jnp.int32, (halo, cw), 0)
    nl = cw // LANES
    rc = next(tt // n for n in range(2, tt) if tt % (n * SUBLANES) == 0 and tt // n <= LRU_CHUNK_ROWS)
    ng = tt // SUBLANES
    assert ng % SUBLANES == 0 and rc % SUBLANES == 0

    def scan_rows(t0):
        s, g0 = divmod(t0, ng)
        return pl.ds(g0 * SUBLANES + s, SUBLANES, stride=SUBLANES)

    for c0 in range(0, tt, rc):
        x = xr_ref[c0:c0 + rc, :]
        prev = halo_ref[...] if c0 == 0 else xr_ref[c0 - halo:c0, :]
        xc = conv_b + x * prm[4:5]
        for j in range(1, CONV_W):
            xs = pltpu.roll(x, j, 0)
            head = jnp.where(row8 < j, pltpu.roll(prev, j, 0), xs[0:halo, :])
            xc = xc + jnp.concatenate([head, xs[halo:, :]], axis=0) * prm[4 + j:5 + j]

        gz = jnp.dot(xc.astype(BF16), w_ref[...], preferred_element_type=F32)
        tr = jnp.tanh(gz[:, :cw] + ga_b)
        ti = jnp.tanh(gz[:, cw:] + gx_b)
        log_a = tr * half_scale + half_scale
        a = jnp.exp(log_a)
        y = -jnp.tanh(log_a) * (a * a + 1.0)
        mult = jnp.where(y > 0.0, y * lax.rsqrt(y), 0.0)
        xh = 0.5 * xc
        bx = mult * (ti * xh + xh)
        for r0 in range(0, rc, SUBLANES):
            dst = scan_rows(c0 + r0)
            for k in range(nl):
                a_ref[k, dst, :] = a[r0:r0 + SUBLANES, k * LANES:(k + 1) * LANES]
                b_ref[k, dst, :] = bx[r0:r0 + SUBLANES, k * LANES:(k + 1) * LANES]
    halo_ref[...] = xr_ref[tt - halo:tt, :]

    def group(ref, g):
        rows = pl.ds(pl.multiple_of(g * SUBLANES, SUBLANES), SUBLANES)
        return jnp.concatenate([ref[k, rows, :] for k in range(nl)], axis=1)

    def put_group(ref, g, val):
        rows = pl.ds(pl.multiple_of(g * SUBLANES, SUBLANES), SUBLANES)
        for k in range(nl):
            ref[k, rows, :] = val[:, k * LANES:(k + 1) * LANES]

    def block_scan(g, state):
        h, cum = state
        ag = group(a_ref, g)
        h = ag * h + group(b_ref, g)
        cum = ag * cum
        put_group(h_ref, g, h)
        put_group(a_ref, g, cum)
        return h, cum

    h, cum = lax.fori_loop(0, ng, block_scan,
                           (jnp.zeros((SUBLANES, cw), F32), jnp.ones((SUBLANES, cw), F32)),
                           unroll=LRU_SCAN_UNROLL)
    e = carry_ref[...]
    entry = []
    for s in range(SUBLANES):
        entry.append(e)
        e = cum[s:s + 1, :] * e + h[s:s + 1, :]
    carry_ref[...] = e
    entry = jnp.concatenate(entry, axis=0)

    def block_fix(g, carry):
        put_group(h_ref, g, group(h_ref, g) + group(a_ref, g) * entry)
        return carry

    lax.fori_loop(0, ng, block_fix, 0, unroll=LRU_SCAN_UNROLL)

    for c0 in range(0, tt, rc):
        hs = jnp.concatenate(
            [jnp.concatenate([h_ref[k, scan_rows(c0 + r0), :] for k in range(nl)], axis=1)
             for r0 in range(0, rc, SUBLANES)], axis=0)
        o_ref[c0:c0 + rc, :] = (hs * jax.nn.gelu(yr_ref[c0:c0 + rc, :])).astype(o_ref.dtype)


def _lru(z_lru, prm, w_gate, layer, batch, seq_len, cast_next=None):
    rows = z_lru.shape[0]
    width = z_lru.shape[1] // 2
    cw = LRU_SUPER
    ns = width // cw
    tt = _pick(seq_len, LRU_TILE_ROWS)
    nt = seq_len // tt
    cast = None
    if cast_next is not None:
        cast = _SideCast(cast_next[0], cast_next[1], batch * ns * nt,
                         lambda b, s, t: (b * ns + s) * nt + t)
        if not cast.ok:
            cast = None
    outs = pl.pallas_call(
        functools.partial(_lru_kernel, cast.n if cast else 0),
        grid=(batch, ns, nt),
        in_specs=[
            pl.BlockSpec((tt, cw), lambda b, s, t: (b * nt + t, s)),
            pl.BlockSpec((tt, cw), lambda b, s, t: (b * nt + t, ns + s)),
            pl.BlockSpec((None, 2 * CONV_W, cw), lambda b, s, t: (layer, 0, s)),
            pl.BlockSpec((None, None, cw, 2 * cw), lambda b, s, t: (layer, s, 0, 0)),
        ] + (cast.in_specs if cast else []),
        out_specs=[pl.BlockSpec((tt, cw), lambda b, s, t: (b * nt + t, s))] + (cast.out_specs if cast else []),
        out_shape=[jax.ShapeDtypeStruct((rows, width), BF16)] + (cast.out_shapes if cast else []),
        scratch_shapes=[
            pltpu.VMEM((SUBLANES, cw), F32),
            pltpu.VMEM((1, cw), F32),
            pltpu.VMEM((cw // LANES, tt, LANES), F32),
            pltpu.VMEM((cw // LANES, tt, LANES), F32),
            pltpu.VMEM((cw // LANES, tt, LANES), F32),
        ],
        compiler_params=_params(("arbitrary", "arbitrary", "arbitrary")),
        name="rglru",
    )(z_lru, z_lru, prm, w_gate, *(cast.views if cast else []))
    if cast:
        return outs[0], cast.finish(outs[1:])
    return outs[0], (None if cast_next is None else [w[cast_next[1]].astype(BF16) for w in cast_next[0]])


def _lru_gate_weights(ga_w, gx_w):
    depth, nb, bw, _ = ga_w.shape
    per = LRU_SUPER // bw
    ns = nb // per

    def dense(w):
        w = w.reshape(depth, ns, per, bw, bw)
        eye = jnp.eye(per, dtype=w.dtype)
        full = jnp.einsum("lspkj,pq->lspkqj", w, eye)
        return full.reshape(depth, ns, per * bw, per * bw)

    return (0.5 * jnp.concatenate([dense(ga_w), dense(gx_w)], axis=-1)).astype(BF16)


def _mix_kernel(oa_ref, ol_ref, wa_ref, wl_ref, ga_ref, gl_ref, o_ref):
    ya = jnp.dot(oa_ref[...], wa_ref[...], preferred_element_type=F32)
    yl = jnp.dot(ol_ref[...], wl_ref[...], preferred_element_type=F32)
    o_ref[...] = (ga_ref[...] * ya + gl_ref[...] * yl).astype(o_ref.dtype)


def _mix(oa, ol, w_ba, w_bl, gates):
    rows, da = oa.shape
    dl = ol.shape[1]
    d = w_ba.shape[-1]
    tm = _pick(rows, (1408, 1056, 768, 384, 128))
    tn = _pick(d, (512, 256, 128))
    nb = d // tn
    return pl.pallas_call(
        _mix_kernel,
        grid=(rows // tm, nb),
        in_specs=[
            pl.BlockSpec((tm, da), lambda i, n: (i, 0)),
            pl.BlockSpec((tm, dl), lambda i, n: (i, 0)),
            pl.BlockSpec((da, tn), lambda i, n: (0, n)),
            pl.BlockSpec((dl, tn), lambda i, n: (0, n)),
            pl.BlockSpec((tm, tn), lambda i, n: (i, n)),
            pl.BlockSpec((tm, tn), lambda i, n: (i, nb + n)),
        ],
        out_specs=pl.BlockSpec((tm, tn), lambda i, n: (i, n)),
        out_shape=jax.ShapeDtypeStruct((rows, d), BF16),
        compiler_params=_params(("arbitrary", "arbitrary")),
        name="branch_mix",
    )(oa, ol, w_ba, w_bl, gates, gates)


def _out_kernel(y_ref, w_ref, h_ref, o_ref):
    o_ref[...] = h_ref[...] + jnp.dot(y_ref[...], w_ref[...], preferred_element_type=F32)


def _out_proj(y, w_o, h):
    rows, d = h.shape
    tm = _pick(rows, (2112, 1056, 768, 384, 128))
    tn = _pick(d, (512, 256, 128))
    return pl.pallas_call(
        _out_kernel,
        grid=(rows // tm, d // tn),
        in_specs=[
            pl.BlockSpec((tm, y.shape[1]), lambda i, n: (i, 0)),
            pl.BlockSpec((y.shape[1], tn), lambda i, n: (0, n)),
            pl.BlockSpec((tm, tn), lambda i, n: (i, n)),
        ],
        out_specs=pl.BlockSpec((tm, tn), lambda i, n: (i, n)),
        out_shape=jax.ShapeDtypeStruct((rows, d), F32),
        compiler_params=_params(("arbitrary", "arbitrary")),
        name="out_proj",
    )(y, w_o, h)


def _final_norm_kernel(h_ref, next_ref, g_ref, o_ref):
    x = jnp.concatenate([h_ref[N_META:, :], next_ref[...]], axis=0)
    ms = jnp.mean(x * x, axis=-1, keepdims=True)
    o_ref[...] = x * lax.rsqrt(ms + EPS) * g_ref[...]


def _final_norm(h, gain, batch, seq_len, seq):
    d = h.shape[1]
    tb = _pick(seq, (512, 256, 128))
    assert tb % N_META == 0 and seq + N_META <= seq_len
    h3 = h.reshape(batch, seq_len, d)
    return pl.pallas_call(
        _final_norm_kernel,
        grid=(batch, seq // tb),
        in_specs=[pl.BlockSpec((None, tb, d), lambda b, j: (b, j, 0)),
                  pl.BlockSpec((None, N_META, d), lambda b, j: (b, (j + 1) * (tb // N_META), 0)),
                  pl.BlockSpec((1, d), lambda b, j: (0, 0))],
        out_specs=pl.BlockSpec((None, tb, d), lambda b, j: (b, j, 0)),
        out_shape=jax.ShapeDtypeStruct((batch, seq, d), F32),
        compiler_params=_params(("arbitrary", "arbitrary")),
        name="final_norm",
    )(h3, h3, gain)


def kernel(x, meta_tokens, norm_ffn1, ffn1_w_gate, ffn1_w_up, ffn1_w_down, norm_mix, w_in, b_gate, lambda_q1, lambda_k1, lambda_q2, lambda_k2, attn_subln, conv_w, conv_b, gate_x_w, gate_x_b, gate_a_w, gate_a_b, lru_a_param, w_branch_attn, w_branch_lru, w_out, norm_ffn2, ffn2_w_gate, ffn2_w_up, ffn2_w_down, final_norm):
    batch, seq, d = x.shape
    depth = w_in.shape[0]
    attn_v = w_branch_attn.shape[1]
    attn_qk = attn_v
    lru_w = w_branch_lru.shape[1]
    seq_real = N_META + seq
    tp = -(-seq_real // SEQ_BLOCK) * SEQ_BLOCK
    rows = batch * tp

    meta = jnp.broadcast_to(meta_tokens.astype(x.dtype)[None], (batch, N_META, d))
    h = jnp.concatenate([meta, x, jnp.zeros((batch, tp - seq_real, d), x.dtype)], axis=1)
    h = h.reshape(rows, d)

    def vec(p):
        return p[:, None, :]

    bf = lambda w: w.astype(BF16)
    ffn1_f32 = (ffn1_w_gate, ffn1_w_up, ffn1_w_down)
    ffn2_f32 = (ffn2_w_gate, ffn2_w_up, ffn2_w_down)
    f1 = [bf(w[0]) for w in ffn1_f32]
    f2 = [bf(w[0]) for w in ffn2_f32]
    w_in_l = bf(w_in[0])
    branch_f32 = (w_branch_attn, w_branch_lru, w_out)
    w_ba, w_bl, w_o = [bf(w[0]) for w in branch_f32]
    w_lru_gate = _lru_gate_weights(gate_a_w, gate_x_w)
    lru_prm = jnp.concatenate([conv_b[:, None], 0.5 * gate_a_b[:, None], 0.5 * gate_x_b[:, None],
                               lru_a_param[:, None], conv_w], axis=1)
    rope = _rope_tables(tp)
    n1, nm, n2 = vec(norm_ffn1), vec(norm_mix), vec(norm_ffn2)
    bg = vec(b_gate)
    lq1, lk1, lq2, lk2 = vec(lambda_q1), vec(lambda_k1), vec(lambda_q2), vec(lambda_k2)
    subln = vec(attn_subln)

    c_v = 2 * attn_qk
    c_lru = c_v + attn_v
    c_gate = c_lru + 2 * lru_w
    for l in range(depth):
        lam_init = 0.8 - 0.6 * math.exp(-0.3 * l)
        more = l + 1 < depth
        h, f1 = _ffn(h, n1, l, f1, (ffn1_f32, l + 1) if more else None)
        xn = _norm(h, nm, l)
        qk = _proj(xn, w_in_l, l, 0, 2 * attn_qk, BF16, "rope", rope, seq_len=tp)
        v = _proj(xn, w_in_l, l, c_v, attn_v, BF16, "plain")
        z_lru = _proj(xn, w_in_l, l, c_lru, 2 * lru_w, F32, "plain")
        gates = _proj(xn, w_in_l, l, c_gate, 2 * d, F32, "gate", bg)
        o_attn, w_in_next = _attention(qk, v, lq1, lk1, lq2, lk2, subln, l, lam_init, batch, tp,
                                       ((w_in,), l + 1) if more else None)
        if more:
            w_in_l = w_in_next[0]
        o_lru, branch_next = _lru(z_lru, lru_prm, w_lru_gate, l, batch, tp,
                                  (branch_f32, l + 1) if more else None)
        y = _mix(o_attn, o_lru, w_ba, w_bl, gates)
        h = _out_proj(y, w_o, h)
        if more:
            w_ba, w_bl, w_o = branch_next
        h, f2 = _ffn(h, n2, l, f2, (ffn2_f32, l + 1) if more else None)
    return _final_norm(h, final_norm[None, :], batch, tp, seq)
```

```python
import functools
import math

import jax
import jax.numpy as jnp
from jax import lax
from jax.experimental import pallas as pl
from jax.experimental.pallas import tpu as pltpu

N_META = 16
HEAD_DIM = 128
V_DIM = 2 * HEAD_DIM
ROT_DIM = HEAD_DIM // 4
ROPE_THETA = 500000.0
CONV_W = 4
LRU_C = 8.0
EPS = 1e-6

LANES = 128
SUBLANES = 8
BF16_ROWS = 16
SEQ_BLOCK = 384
ATTN_CHUNK = 2 * SEQ_BLOCK
ATTN_ROWS_PER_DOT = SEQ_BLOCK
LRU_SUPER = 640
LRU_TILE_ROWS = (1408, 384)
LRU_CHUNK_ROWS = 352
LRU_SCAN_UNROLL = 8
VMEM_LIMIT_BYTES = 60 * 1024 * 1024
NEG_BIG = -1e30

BF16 = jnp.bfloat16
F32 = jnp.float32


def _pick(n, prefs):
    for p in prefs:
        if p <= n and n % p == 0:
            return p
    return n


def _params(semantics):
    return pltpu.CompilerParams(dimension_semantics=semantics, vmem_limit_bytes=VMEM_LIMIT_BYTES)


def _sigmoid(x):
    return 0.5 * jnp.tanh(0.5 * x) + 0.5


def _rms_norm_bf16(x, gain):
    ms = jnp.mean(x * x, axis=-1, keepdims=True)
    return (x * lax.rsqrt(ms + EPS) * gain).astype(BF16)


class _SideCast:
    def __init__(self, stacked, layer, n_steps, step_of):
        self.views = list(stacked)
        self.n = len(stacked)
        self.in_specs, self.out_specs, self.out_shapes = [], [], []
        self.ok = True
        for w in stacked:
            _, rows, cols = w.shape
            fits = [rb for rb in range(BF16_ROWS, rows + 1, BF16_ROWS)
                    if rows % rb == 0 and rb * n_steps >= rows]
            if not fits:
                self.ok = False
                continue
            rb = fits[0]
            last = rows // rb - 1
            self.in_specs.append(pl.BlockSpec(
                (None, rb, cols), lambda *g, last=last: (layer, jnp.minimum(step_of(*g), last), 0)))
            self.out_specs.append(pl.BlockSpec(
                (rb, cols), lambda *g, last=last: (jnp.minimum(step_of(*g), last), 0)))
            self.out_shapes.append(jax.ShapeDtypeStruct((rows, cols), BF16))

    @staticmethod
    def run(in_refs, out_refs):
        for src, dst in zip(in_refs, out_refs):
            dst[...] = src[...].astype(BF16)

    def finish(self, outs):
        return list(outs)


def _ffn_kernel(n_cast, h_ref, gain_ref, wg_ref, wu_ref, wd_ref, *rest):
    cast_in, o_ref = rest[:n_cast], rest[n_cast]
    cast_out, xn_ref = rest[n_cast + 1:2 * n_cast + 1], rest[-1]
    _SideCast.run(cast_in, cast_out)
    f = pl.program_id(1)

    @pl.when(f == 0)
    def _():
        x = h_ref[...]
        xn_ref[...] = _rms_norm_bf16(x, gain_ref[...])
        o_ref[...] = x

    xn = xn_ref[...]
    g = jnp.dot(xn, wg_ref[...], preferred_element_type=F32)
    u = jnp.dot(xn, wu_ref[...], preferred_element_type=F32)
    a = (0.5 * (g * _sigmoid(g)) * u).astype(BF16)
    o_ref[...] += jnp.dot(a, wd_ref[...], preferred_element_type=F32)


def _ffn(h, gain, layer, weights, cast_next=None):
    wg, wu, wd = weights
    rows, d = h.shape
    ff = wg.shape[-1]
    tm = _pick(rows, (1056, 768, 512, 384, 256, 128))
    tf = _pick(ff, (512, 256, 128))
    nf = ff // tf
    cast = None
    if cast_next is not None:
        cast = _SideCast(cast_next[0], cast_next[1], (rows // tm) * nf, lambda i, f: i * nf + f)
        if not cast.ok:
            cast = None
    n_cast = cast.n if cast else 0
    outs = pl.pallas_call(
        functools.partial(_ffn_kernel, n_cast),
        grid=(rows // tm, nf),
        in_specs=[
            pl.BlockSpec((tm, d), lambda i, f: (i, 0)),
            pl.BlockSpec((None, 1, d), lambda i, f: (layer, 0, 0)),
            pl.BlockSpec((d, tf), lambda i, f: (0, f)),
            pl.BlockSpec((d, tf), lambda i, f: (0, f)),
            pl.BlockSpec((tf, d), lambda i, f: (f, 0)),
        ] + (cast.in_specs if cast else []),
        out_specs=[pl.BlockSpec((tm, d), lambda i, f: (i, 0))] + (cast.out_specs if cast else []),
        out_shape=[jax.ShapeDtypeStruct((rows, d), F32)] + (cast.out_shapes if cast else []),
        scratch_shapes=[pltpu.VMEM((tm, d), BF16)],
        compiler_params=_params(("arbitrary", "arbitrary")),
        name="ffn",
    )(h, gain, wg, wu, wd, *(cast.views if cast else []))
    if cast:
        return outs[0], cast.finish(outs[1:])
    fallback = None if cast_next is None else [w[cast_next[1]].astype(BF16) for w in cast_next[0]]
    return outs[0], fallback


def _norm_kernel(h_ref, gain_ref, o_ref):
    o_ref[...] = _rms_norm_bf16(h_ref[...], gain_ref[...])


def _norm(h, gain, layer):
    rows, d = h.shape
    tm = _pick(rows, (1056, 768, 384, 128))
    return pl.pallas_call(
        _norm_kernel,
        grid=(rows // tm,),
        in_specs=[pl.BlockSpec((tm, d), lambda i: (i, 0)),
                  pl.BlockSpec((None, 1, d), lambda i: (layer, 0, 0))],
        out_specs=pl.BlockSpec((tm, d), lambda i: (i, 0)),
        out_shape=jax.ShapeDtypeStruct((rows, d), BF16),
        compiler_params=_params(("arbitrary",)),
        name="mix_norm",
    )(h, gain)


def _proj_kernel(kind, n_cast, xn_ref, w_ref, *rest):
    n_extra = 0 if kind == "plain" else 1
    cast_in = rest[n_extra:n_extra + n_cast]
    o_ref = rest[n_extra + n_cast]
    _SideCast.run(cast_in, rest[n_extra + n_cast + 1:])
    tm, tn = o_ref.shape
    chunks = {"rope": 6 if tm % (6 * BF16_ROWS) == 0 else 4, "gate": 2, "plain": 1}[kind]
    cm = tm // chunks
    for r in range(chunks):
        rows = slice(r * cm, (r + 1) * cm)
        z = jnp.dot(xn_ref[rows, :], w_ref[...], preferred_element_type=F32)
        if kind == "rope":
            tab_ref = rest[0]
            reps = tn // LANES
            cos = jnp.concatenate([tab_ref[0, rows, :]] * reps, axis=1)
            sin_lo = jnp.concatenate([tab_ref[1, rows, :]] * reps, axis=1)
            sin_hi = jnp.concatenate([tab_ref[2, rows, :]] * reps, axis=1)
            half = ROT_DIM // 2
            z = z * cos + pltpu.roll(z, half, 1) * sin_lo + pltpu.roll(z, tn - half, 1) * sin_hi
        elif kind == "gate":
            z = _sigmoid(z + rest[0][...])
        o_ref[rows, :] = z.astype(o_ref.dtype)


def _proj(xn, w, layer, col0, ncols, out_dtype, kind, extra=None, seq_len=None, cast_next=None):
    rows, d = xn.shape
    tm = _pick(seq_len, (2112, 1408, 768, 384, 128)) if seq_len else _pick(rows, (2112, 1408, 768, 384, 128))
    tn = _pick(ncols // 2 if kind == "rope" else ncols, (512, 256, 128))
    assert col0 % tn == 0
    cb0 = col0 // tn
    in_specs = [
        pl.BlockSpec((tm, d), lambda i, n: (i, 0)),
        pl.BlockSpec((d, tn), lambda i, n: (0, cb0 + n)),
    ]
    args = [xn, w]
    if kind == "rope":
        tiles_per_seq = seq_len // tm
        q_blocks = ncols // 2 // tn
        in_specs.append(pl.BlockSpec((None, 3, tm, LANES),
                                     lambda i, n: (n // q_blocks, 0, i % tiles_per_seq, 0)))
        args.append(extra)
    elif kind == "gate":
        in_specs.append(pl.BlockSpec((None, 1, tn), lambda i, n: (layer, 0, n)))
        args.append(extra)
    nn = ncols // tn
    cast = None
    if cast_next is not None:
        cast = _SideCast(cast_next[0], cast_next[1], (rows // tm) * nn, lambda i, n: i * nn + n)
        if not cast.ok:
            cast = None
    outs = pl.pallas_call(
        functools.partial(_proj_kernel, kind, cast.n if cast else 0),
        grid=(rows // tm, nn),
        in_specs=in_specs + (cast.in_specs if cast else []),
        out_specs=[pl.BlockSpec((tm, tn), lambda i, n: (i, n))] + (cast.out_specs if cast else []),
        out_shape=[jax.ShapeDtypeStruct((rows, ncols), out_dtype)] + (cast.out_shapes if cast else []),
        compiler_params=_params(("arbitrary", "arbitrary")),
        name="proj_" + kind,
    )(*args, *(cast.views if cast else []))
    if cast_next is None:
        return outs[0]
    if cast:
        return outs[0], cast.finish(outs[1:])
    return outs[0], [w[cast_next[1]].astype(BF16) for w in cast_next[0]]


def _rope_tables(seq_len):
    half = ROT_DIM // 2
    inv_freq = ROPE_THETA ** (-jnp.arange(0, ROT_DIM, 2, dtype=F32) / ROT_DIM)
    ang = jnp.arange(seq_len, dtype=F32)[:, None] * inv_freq[None, :]
    cos, sin = jnp.cos(ang), jnp.sin(ang)
    ones = jnp.ones((seq_len, LANES - ROT_DIM), F32)
    zeros = jnp.zeros((seq_len, LANES - ROT_DIM), F32)
    zh = jnp.zeros((seq_len, half), F32)
    c = jnp.concatenate([cos, cos, ones], axis=1)
    s_lo = jnp.concatenate([zh, sin, zeros], axis=1)
    s_hi = jnp.concatenate([-sin, zh, zeros], axis=1)
    k_tab = jnp.stack([c, s_lo, s_hi])
    return jnp.stack([k_tab * ((HEAD_DIM ** -0.5) * math.log2(math.e)), k_tab])


def _attn_stages(k_ref, v_ref, qs, m_ref, l_ref, acc_ref, p_refs, a_refs):
    tq = qs[0].shape[0]
    rows_per_dot = min(tq, ATTN_ROWS_PER_DOT)

    def score_stage(start, width, masked, slot):
        kb = k_ref[pl.ds(start, width), :]
        for c in range(2):
            kc = kb[:, c * HEAD_DIM:(c + 1) * HEAD_DIM]
            for r0 in range(0, tq, rows_per_dot):
                rows = slice(r0, r0 + rows_per_dot)
                kw = min(width, r0 + rows_per_dot) if masked else width
                s = lax.dot_general(qs[c][rows], kc[0:kw], (((1,), (1,)), ((), ())),
                                    preferred_element_type=F32)
                if masked:
                    row = lax.broadcasted_iota(jnp.int32, s.shape, 0) + r0
                    col = lax.broadcasted_iota(jnp.int32, s.shape, 1)
                    s = jnp.where(col <= row, s, NEG_BIG)
                m_old = m_ref[c, rows, :]
                m_new = jnp.maximum(m_old, jnp.max(s, axis=-1, keepdims=True))
                alpha = jnp.exp2(m_old - m_new)
                p = jnp.exp2(s - jnp.tile(m_new, (1, kw // LANES)))
                l_ref[c, rows, :] = alpha * l_ref[c, rows, :] + jnp.sum(p, axis=-1, keepdims=True)
                m_ref[c, rows, :] = m_new
                a_refs[slot][c, rows, :] = alpha
                p_refs[slot][c, rows, 0:kw] = p.astype(BF16)
                if kw < width:
                    p_refs[slot][c, rows, kw:width] = jnp.zeros((rows_per_dot, width - kw), BF16)

    def value_stage(start, width, slot):
        vb = v_ref[pl.ds(start, width), :]
        for c in range(2):
            for r0 in range(0, tq, rows_per_dot):
                rows = slice(r0, r0 + rows_per_dot)
                pv = jnp.dot(p_refs[slot][c, rows, 0:width], vb, preferred_element_type=F32)
                acc_ref[c, rows, :] = (jnp.tile(a_refs[slot][c, rows, :], (1, V_DIM // LANES))
                                       * acc_ref[c, rows, :] + pv)

    return score_stage, value_stage


def _attn_init(m_ref, l_ref, acc_ref):
    m_ref[...] = jnp.full(m_ref.shape, NEG_BIG, F32)
    l_ref[...] = jnp.zeros(l_ref.shape, F32)
    acc_ref[...] = jnp.zeros(acc_ref.shape, F32)


def _attn_finish(lam_init, lq1_ref, lk1_ref, lq2_ref, lk2_ref, subln_ref, o_ref, l_ref, acc_ref):
    lam = (jnp.exp(jnp.sum(lq1_ref[...] * lk1_ref[...], keepdims=True))
           - jnp.exp(jnp.sum(lq2_ref[...] * lk2_ref[...], keepdims=True)) + lam_init)
    reps = V_DIM // LANES
    o = (acc_ref[0] * jnp.tile(1.0 / l_ref[0], (1, reps))
         - lam * (acc_ref[1] * jnp.tile(1.0 / l_ref[1], (1, reps))))
    o = o * lax.rsqrt(jnp.mean(o * o, axis=-1, keepdims=True) + EPS)
    o = o * subln_ref[...] * (1.0 - lam_init)
    o_ref[...] = o.astype(o_ref.dtype)


def _attn_main_kernel(lam_init, n_cast, q_ref, k_ref, v_ref, lq1_ref, lk1_ref, lq2_ref, lk2_ref,
                      subln_ref, *rest):
    cast_in, o_ref, cast_out = rest[:n_cast], rest[n_cast], rest[n_cast + 1:2 * n_cast + 1]
    m_ref, l_ref, acc_ref, p0_ref, p1_ref, a0_ref, a1_ref = rest[2 * n_cast + 1:]
    _SideCast.run(cast_in, cast_out)
    i = pl.program_id(2)
    tq = q_ref.shape[0]
    q = q_ref[...]
    score_stage, value_stage = _attn_stages(k_ref, v_ref, (q[:, :HEAD_DIM], q[:, HEAD_DIM:]),
                                            m_ref, l_ref, acc_ref, (p0_ref, p1_ref), (a0_ref, a1_ref))
    _attn_init(m_ref, l_ref, acc_ref)
    score_stage(pl.multiple_of(i * tq, tq), tq, True, 0)

    def chunk_start(n):
        return pl.multiple_of(jnp.where(n == 0, i, n - 1) * tq, tq)

    def body(n, carry):
        for slot in range(2):
            @pl.when(n % 2 == slot)
            def _():
                score_stage(chunk_start(n), tq, False, slot)
                value_stage(chunk_start(n - 1), tq, 1 - slot)
        return carry

    lax.fori_loop(1, i + 1, body, 0)
    for slot in range(2):
        @pl.when(i % 2 == slot)
        def _():
            value_stage(chunk_start(i), tq, slot)
    _attn_finish(lam_init, lq1_ref, lk1_ref, lq2_ref, lk2_ref, subln_ref, o_ref, l_ref, acc_ref)


def _attn_tail_kernel(lam_init, tk, q_ref, k_ref, v_ref, lq1_ref, lk1_ref, lq2_ref, lk2_ref, subln_ref,
                      o_in_ref, o_ref, m_ref, l_ref, acc_ref, p0_ref, p1_ref, a0_ref, a1_ref):
    del o_in_ref
    tq = q_ref.shape[0]
    n_full = (k_ref.shape[0] - tq) // tk
    q = q_ref[...]
    score_stage, value_stage = _attn_stages(k_ref, v_ref, (q[:, :HEAD_DIM], q[:, HEAD_DIM:]),
                                            m_ref, l_ref, acc_ref, (p0_ref, p1_ref), (a0_ref, a1_ref))
    _attn_init(m_ref, l_ref, acc_ref)
    chunks = [(n_full * tk, tq)] + [(j * tk, tk) for j in range(n_full)]
    score_stage(chunks[0][0], chunks[0][1], True, 0)
    for n in range(1, len(chunks)):
        score_stage(chunks[n][0], chunks[n][1], False, n % 2)
        value_stage(chunks[n - 1][0], chunks[n - 1][1], (n - 1) % 2)
    value_stage(chunks[-1][0], chunks[-1][1], (len(chunks) - 1) % 2)
    _attn_finish(lam_init, lq1_ref, lk1_ref, lq2_ref, lk2_ref, subln_ref, o_ref, l_ref, acc_ref)


def _attn_scratch(tq, tk):
    return [
        pltpu.VMEM((2, tq, LANES), F32),
        pltpu.VMEM((2, tq, LANES), F32),
        pltpu.VMEM((2, tq, V_DIM), F32),
        pltpu.VMEM((2, tq, tk), BF16),
        pltpu.VMEM((2, tq, tk), BF16),
        pltpu.VMEM((2, tq, LANES), F32),
        pltpu.VMEM((2, tq, LANES), F32),
    ]


def _attention(qk, v, lq1, lk1, lq2, lk2, subln, layer, lam_init, batch, seq_len, cast_next=None):
    n_heads = v.shape[1] // V_DIM
    tk = ATTN_CHUNK
    nq = seq_len // tk
    rem = seq_len - nq * tk
    qk3 = qk.reshape(batch, seq_len, qk.shape[1])
    v3 = v.reshape(batch, seq_len, v.shape[1])
    out_shape = jax.ShapeDtypeStruct((batch, seq_len, n_heads * V_DIM), BF16)

    def kv_param_specs(grid_rank):
        def at(*idx):
            return (lambda b, h, i: idx_of(b, h, idx)) if grid_rank == 3 else (lambda b, h: idx_of(b, h, idx))

        def idx_of(b, h, idx):
            return tuple(f(b, h) if callable(f) else f for f in idx)

        lam_spec = pl.BlockSpec((None, 1, HEAD_DIM), at(layer, 0, 0))
        return [pl.BlockSpec((None, seq_len, V_DIM), at(lambda b, h: b, 0, lambda b, h: n_heads + h)),
                pl.BlockSpec((None, seq_len, V_DIM), at(lambda b, h: b, 0, lambda b, h: h)),
                lam_spec, lam_spec, lam_spec, lam_spec,
                pl.BlockSpec((None, 1, V_DIM), at(layer, 0, 0))]

    cast = None
    if cast_next is not None:
        cast = _SideCast(cast_next[0], cast_next[1], batch * n_heads * nq,
                         lambda b, h, i: (b * n_heads + h) * nq + i)
        if not cast.ok:
            cast = None
    outs = pl.pallas_call(
        functools.partial(_attn_main_kernel, lam_init, cast.n if cast else 0),
        grid=(batch, n_heads, nq),
        in_specs=([pl.BlockSpec((None, tk, V_DIM), lambda b, h, i: (b, i, h))] + kv_param_specs(3)
                  + (cast.in_specs if cast else [])),
        out_specs=[pl.BlockSpec((None, tk, V_DIM), lambda b, h, i: (b, i, h))] + (cast.out_specs if cast else []),
        out_shape=[out_shape] + (cast.out_shapes if cast else []),
        scratch_shapes=_attn_scratch(tk, tk),
        compiler_params=_params(("arbitrary", "arbitrary", "arbitrary")),
        name="attn_main",
    )(qk3, qk3, v3, lq1, lk1, lq2, lk2, subln, *(cast.views if cast else []))
    out = outs[0]
    if cast:
        cast_out = cast.finish(outs[1:])
    else:
        cast_out = None if cast_next is None else [w[cast_next[1]].astype(BF16) for w in cast_next[0]]
    if rem:
        assert seq_len % rem == 0
        last = seq_len // rem - 1
        out = pl.pallas_call(
            functools.partial(_attn_tail_kernel, lam_init, tk),
            grid=(batch, n_heads),
            in_specs=([pl.BlockSpec((None, rem, V_DIM), lambda b, h: (b, last, h))] + kv_param_specs(2)
                      + [pl.BlockSpec(memory_space=pl.ANY)]),
            out_specs=pl.BlockSpec((None, rem, V_DIM), lambda b, h: (b, last, h)),
            out_shape=out_shape,
            scratch_shapes=_attn_scratch(rem, tk),
            input_output_aliases={8: 0},
            compiler_params=_params(("arbitrary", "arbitrary")),
            name="attn_tail",
        )(qk3, qk3, v3, lq1, lk1, lq2, lk2, subln, out)
    return out.reshape(batch * seq_len, n_heads * V_DIM), cast_out


def _lru_kernel(n_cast, xr_ref, yr_ref, p_ref, w_ref, *rest):
    cast_in, o_ref, cast_out = rest[:n_cast], rest[n_cast], rest[n_cast + 1:2 * n_cast + 1]
    halo_ref, carry_ref, a_ref, b_ref, h_ref = rest[2 * n_cast + 1:]
    _SideCast.run(cast_in, cast_out)
    t = pl.program_id(2)
    tt, cw = xr_ref.shape
    halo = SUBLANES

    @pl.when(t == 0)
    def _():
        halo_ref[...] = jnp.zeros(halo_ref.shape, F32)
        carry_ref[...] = jnp.zeros(carry_ref.shape, F32)

    prm = p_ref[...]
    conv_b, ga_b, gx_b = prm[0:1], prm[1:2], prm[2:3]
    half_scale = (0.5 * LRU_C) * jax.nn.log_sigmoid(prm[3:4])
    row8 = lax.broadcasted_iota(jnp.int32, (halo, cw), 0)
    nl = cw // LANES
    rc = next(tt // n for n in range(2, tt) if tt % (n * SUBLANES) == 0 and tt // n <= LRU_CHUNK_ROWS)
    ng = tt // SUBLANES
    assert ng % SUBLANES == 0 and rc % SUBLANES == 0

    def scan_rows(t0):
        s, g0 = divmod(t0, ng)
        return pl.ds(g0 * SUBLANES + s, SUBLANES, stride=SUBLANES)

    for c0 in range(0, tt, rc):
        x = xr_ref[c0:c0 + rc, :]
        prev = halo_ref[...] if c0 == 0 else xr_ref[c0 - halo:c0, :]
        xc = conv_b + x * prm[4:5]
        for j in range(1, CONV_W):
            xs = pltpu.roll(x, j, 0)
            head = jnp.where(row8 < j, pltpu.roll(prev, j, 0), xs[0:halo, :])
            xc = xc + jnp.concatenate([head, xs[halo:, :]], axis=0) * prm[4 + j:5 + j]

        gz = jnp.dot(xc.astype(BF16), w_ref[...], preferred_element_type=F32)
        tr = jnp.tanh(gz[:, :cw] + ga_b)
        ti = jnp.tanh(gz[:, cw:] + gx_b)
        log_a = tr * half_scale + half_scale
        a = jnp.exp(log_a)
        y = -jnp.tanh(log_a) * (a * a + 1.0)
        mult = jnp.where(y > 0.0, y * lax.rsqrt(y), 0.0)
        xh = 0.5 * xc
        bx = mult * (ti * xh + xh)
        for r0 in range(0, rc, SUBLANES):
            dst = scan_rows(c0 + r0)
            for k in range(nl):
                a_ref[k, dst, :] = a[r0:r0 + SUBLANES, k * LANES:(k + 1) * LANES]
                b_ref[k, dst, :] = bx[r0:r0 + SUBLANES, k * LANES:(k + 1) * LANES]
    halo_ref[...] = xr_ref[tt - halo:tt, :]

    def group(ref, g):
        rows = pl.ds(pl.multiple_of(g * SUBLANES, SUBLANES), SUBLANES)
        return jnp.concatenate([ref[k, rows, :] for k in range(nl)], axis=1)

    def put_group(ref, g, val):
        rows = pl.ds(pl.multiple_of(g * SUBLANES, SUBLANES), SUBLANES)
        for k in range(nl):
            ref[k, rows, :] = val[:, k * LANES:(k + 1) * LANES]

    def block_scan(g, state):
        h, cum = state
        ag = group(a_ref, g)
        h = ag * h + group(b_ref, g)
        cum = ag * cum
        put_group(h_ref, g, h)
        put_group(a_ref, g, cum)
        return h, cum

    h, cum = lax.fori_loop(0, ng, block_scan,
                           (jnp.zeros((SUBLANES, cw), F32), jnp.ones((SUBLANES, cw), F32)),
                           unroll=LRU_SCAN_UNROLL)
    e = carry_ref[...]
    entry = []
    for s in range(SUBLANES):
        entry.append(e)
        e = cum[s:s + 1, :] * e + h[s:s + 1, :]
    carry_ref[...] = e
    entry = jnp.concatenate(entry, axis=0)

    def block_fix(g, carry):
        put_group(h_ref, g, group(h_ref, g) + group(a_ref, g) * entry)
        return carry

    lax.fori_loop(0, ng, block_fix, 0, unroll=LRU_SCAN_UNROLL)

    for c0 in range(0, tt, rc):
        hs = jnp.concatenate(
            [jnp.concatenate([h_ref[k, scan_rows(c0 + r0), :] for k in range(nl)], axis=1)
             for r0 in range(0, rc, SUBLANES)], axis=0)
        o_ref[c0:c0 + rc, :] = (hs * jax.nn.gelu(yr_ref[c0:c0 + rc, :])).astype(o_ref.dtype)


def _lru(z_lru, prm, w_gate, layer, batch, seq_len, cast_next=None):
    rows = z_lru.shape[0]
    width = z_lru.shape[1] // 2
    cw = LRU_SUPER
    ns = width // cw
    tt = _pick(seq_len, LRU_TILE_ROWS)
    nt = seq_len // tt
    cast = None
    if cast_next is not None:
        cast = _SideCast(cast_next[0], cast_next[1], batch * ns * nt,
                         lambda b, s, t: (b * ns + s) * nt + t)
        if not cast.ok:
            cast = None
    outs = pl.pallas_call(
        functools.partial(_lru_kernel, cast.n if cast else 0),
        grid=(batch, ns, nt),
        in_specs=[
            pl.BlockSpec((tt, cw), lambda b, s, t: (b * nt + t, s)),
            pl.BlockSpec((tt, cw), lambda b, s, t: (b * nt + t, ns + s)),
            pl.BlockSpec((None, 2 * CONV_W, cw), lambda b, s, t: (layer, 0, s)),
            pl.BlockSpec((None, None, cw, 2 * cw), lambda b, s, t: (layer, s, 0, 0)),
        ] + (cast.in_specs if cast else []),
        out_specs=[pl.BlockSpec((tt, cw), lambda b, s, t: (b * nt + t, s))] + (cast.out_specs if cast else []),
        out_shape=[jax.ShapeDtypeStruct((rows, width), BF16)] + (cast.out_shapes if cast else []),
        scratch_shapes=[
            pltpu.VMEM((SUBLANES, cw), F32),
            pltpu.VMEM((1, cw), F32),
            pltpu.VMEM((cw // LANES, tt, LANES), F32),
            pltpu.VMEM((cw // LANES, tt, LANES), F32),
            pltpu.VMEM((cw // LANES, tt, LANES), F32),
        ],
        compiler_params=_params(("arbitrary", "arbitrary", "arbitrary")),
        name="rglru",
    )(z_lru, z_lru, prm, w_gate, *(cast.views if cast else []))
    if cast:
        return outs[0], cast.finish(outs[1:])
    return outs[0], (None if cast_next is None else [w[cast_next[1]].astype(BF16) for w in cast_next[0]])


def _lru_gate_weights(ga_w, gx_w):
    depth, nb, bw, _ = ga_w.shape
    per = LRU_SUPER // bw
    ns = nb // per

    def dense(w):
        w = (0.5 * w).astype(BF16).reshape(depth, ns, per, bw, bw)
        rows = [jnp.pad(w[:, :, p], ((0, 0), (0, 0), (0, 0), (p * bw, (per - 1 - p) * bw)))
                for p in range(per)]
        return jnp.concatenate(rows, axis=2)

    return jnp.concatenate([dense(ga_w), dense(gx_w)], axis=-1)


def _mix_kernel(oa_ref, ol_ref, wa_ref, wl_ref, ga_ref, gl_ref, o_ref):
    ya = jnp.dot(oa_ref[...], wa_ref[...], preferred_element_type=F32)
    yl = jnp.dot(ol_ref[...], wl_ref[...], preferred_element_type=F32)
    o_ref[...] = (ga_ref[...] * ya + gl_ref[...] * yl).astype(o_ref.dtype)


def _mix(oa, ol, w_ba, w_bl, gates):
    rows, da = oa.shape
    dl = ol.shape[1]
    d = w_ba.shape[-1]
    tm = _pick(rows, (1408, 1056, 768, 384, 128))
    tn = _pick(d, (512, 256, 128))
    nb = d // tn
    return pl.pallas_call(
        _mix_kernel,
        grid=(rows // tm, nb),
        in_specs=[
            pl.BlockSpec((tm, da), lambda i, n: (i, 0)),
            pl.BlockSpec((tm, dl), lambda i, n: (i, 0)),
            pl.BlockSpec((da, tn), lambda i, n: (0, n)),
            pl.BlockSpec((dl, tn), lambda i, n: (0, n)),
            pl.BlockSpec((tm, tn), lambda i, n: (i, n)),
            pl.BlockSpec((tm, tn), lambda i, n: (i, nb + n)),
        ],
        out_specs=pl.BlockSpec((tm, tn), lambda i, n: (i, n)),
        out_shape=jax.ShapeDtypeStruct((rows, d), BF16),
        compiler_params=_params(("arbitrary", "arbitrary")),
        name="branch_mix",
    )(oa, ol, w_ba, w_bl, gates, gates)


def _out_kernel(y_ref, w_ref, h_ref, o_ref):
    o_ref[...] = h_ref[...] + jnp.dot(y_ref[...], w_ref[...], preferred_element_type=F32)


def _out_proj(y, w_o, h):
    rows, d = h.shape
    tm = _pick(rows, (2112, 1056, 768, 384, 128))
    tn = _pick(d, (512, 256, 128))
    return pl.pallas_call(
        _out_kernel,
        grid=(rows // tm, d // tn),
        in_specs=[
            pl.BlockSpec((tm, y.shape[1]), lambda i, n: (i, 0)),
            pl.BlockSpec((y.shape[1], tn), lambda i, n: (0, n)),
            pl.BlockSpec((tm, tn), lambda i, n: (i, n)),
        ],
        out_specs=pl.BlockSpec((tm, tn), lambda i, n: (i, n)),
        out_shape=jax.ShapeDtypeStruct((rows, d), F32),
        compiler_params=_params(("arbitrary", "arbitrary")),
        name="out_proj",
    )(y, w_o, h)


def _final_norm_kernel(h_ref, next_ref, g_ref, o_ref):
    x = jnp.concatenate([h_ref[N_META:, :], next_ref[...]], axis=0)
    ms = jnp.mean(x * x, axis=-1, keepdims=True)
    o_ref[...] = x * lax.rsqrt(ms + EPS) * g_ref[...]


def _final_norm(h, gain, batch, seq_len, seq):
    d = h.shape[1]
    tb = _pick(seq, (512, 256, 128))
    assert tb % N_META == 0 and seq + N_META <= seq_len
    h3 = h.reshape(batch, seq_len, d)
    return pl.pallas_call(
        _final_norm_kernel,
        grid=(batch, seq // tb),
        in_specs=[pl.BlockSpec((None, tb, d), lambda b, j: (b, j, 0)),
                  pl.BlockSpec((None, N_META, d), lambda b, j: (b, (j + 1) * (tb // N_META), 0)),
                  pl.BlockSpec((1, d), lambda b, j: (0, 0))],
        out_specs=pl.BlockSpec((None, tb, d), lambda b, j: (b, j, 0)),
        out_shape=jax.ShapeDtypeStruct((batch, seq, d), F32),
        compiler_params=_params(("arbitrary", "arbitrary")),
        name="final_norm",
    )(h3, h3, gain)


def kernel(x, meta_tokens, norm_ffn1, ffn1_w_gate, ffn1_w_up, ffn1_w_down, norm_mix, w_in, b_gate, lambda_q1, lambda_k1, lambda_q2, lambda_k2, attn_subln, conv_w, conv_b, gate_x_w, gate_x_b, gate_a_w, gate_a_b, lru_a_param, w_branch_attn, w_branch_lru, w_out, norm_ffn2, ffn2_w_gate, ffn2_w_up, ffn2_w_down, final_norm):
    batch, seq, d = x.shape
    depth = w_in.shape[0]
    attn_v = w_branch_attn.shape[1]
    attn_qk = attn_v
    lru_w = w_branch_lru.shape[1]
    seq_real = N_META + seq
    tp = -(-seq_real // SEQ_BLOCK) * SEQ_BLOCK
    rows = batch * tp

    meta = jnp.broadcast_to(meta_tokens.astype(x.dtype)[None], (batch, N_META, d))
    h = jnp.concatenate([meta, x, jnp.zeros((batch, tp - seq_real, d), x.dtype)], axis=1)
    h = h.reshape(rows, d)

    def vec(p):
        return p[:, None, :]

    bf = lambda w: w.astype(BF16)
    ffn1_f32 = (ffn1_w_gate, ffn1_w_up, ffn1_w_down)
    ffn2_f32 = (ffn2_w_gate, ffn2_w_up, ffn2_w_down)
    f1 = [bf(w[0]) for w in ffn1_f32]
    f2 = None
    w_in_l = bf(w_in[0])
    branch_f32 = (w_branch_attn, w_branch_lru, w_out)
    w_ba = w_bl = w_o = None
    w_lru_gate = _lru_gate_weights(gate_a_w, gate_x_w)
    lru_prm = jnp.concatenate([conv_b[:, None], 0.5 * gate_a_b[:, None], 0.5 * gate_x_b[:, None],
                               lru_a_param[:, None], conv_w], axis=1)
    rope = _rope_tables(tp)
    n1, nm, n2 = vec(norm_ffn1), vec(norm_mix), vec(norm_ffn2)
    bg = vec(b_gate)
    lq1, lk1, lq2, lk2 = vec(lambda_q1), vec(lambda_k1), vec(lambda_q2), vec(lambda_k2)
    subln = vec(attn_subln)

    c_v = 2 * attn_qk
    c_lru = c_v + attn_v
    c_gate = c_lru + 2 * lru_w
    for l in range(depth):
        lam_init = 0.8 - 0.6 * math.exp(-0.3 * l)
        more = l + 1 < depth
        h, f1 = _ffn(h, n1, l, f1, (ffn1_f32, l + 1) if more else None)
        xn = _norm(h, nm, l)
        qk = _proj(xn, w_in_l, l, 0, 2 * attn_qk, BF16, "rope", rope, seq_len=tp)
        if l == 0:
            v, (w_ba, w_bl, w_o) = _proj(xn, w_in_l, l, c_v, attn_v, BF16, "plain",
                                         cast_next=(branch_f32, 0))
            gates, f2 = _proj(xn, w_in_l, l, c_gate, 2 * d, F32, "gate", bg, cast_next=(ffn2_f32, 0))
        else:
            v = _proj(xn, w_in_l, l, c_v, attn_v, BF16, "plain")
            gates = _proj(xn, w_in_l, l, c_gate, 2 * d, F32, "gate", bg)
        z_lru = _proj(xn, w_in_l, l, c_lru, 2 * lru_w, F32, "plain")
        o_attn, w_in_next = _attention(qk, v, lq1, lk1, lq2, lk2, subln, l, lam_init, batch, tp,
                                       ((w_in,), l + 1) if more else None)
        if more:
            w_in_l = w_in_next[0]
        o_lru, branch_next = _lru(z_lru, lru_prm, w_lru_gate, l, batch, tp,
                                  (branch_f32, l + 1) if more else None)
        y = _mix(o_attn, o_lru, w_ba, w_bl, gates)
        h = _out_proj(y, w_o, h)
        if more:
            w_ba, w_bl, w_o = branch_next
        h, f2 = _ffn(h, n2, l, f2, (ffn2_f32, l + 1) if more else None)
    return _final_norm(h, final_norm[None, :], batch, tp, seq)
```

```python
import functools
import math

import jax
import jax.numpy as jnp
from jax import lax
from jax.experimental import pallas as pl
from jax.experimental.pallas import tpu as pltpu

N_META = 16
HEAD_DIM = 128
V_DIM = 2 * HEAD_DIM
ROT_DIM = HEAD_DIM // 4
ROPE_THETA = 500000.0
CONV_W = 4
LRU_C = 8.0
EPS = 1e-6

LANES = 128
SUBLANES = 8
BF16_ROWS = 16
SEQ_BLOCK = 384
ATTN_CHUNK = 2 * SEQ_BLOCK
ATTN_ROWS_PER_DOT = SEQ_BLOCK
LRU_SUPER = 640
LRU_TILE_ROWS = (1408, 384)
LRU_CHUNK_ROWS = 352
LRU_SCAN_UNROLL = 8
VMEM_LIMIT_BYTES = 60 * 1024 * 1024
NEG_BIG = -1e30

BF16 = jnp.bfloat16
F32 = jnp.float32


def _pick(n, prefs):
    for p in prefs:
        if p <= n and n % p == 0:
            return p
    return n


def _params(semantics):
    return pltpu.CompilerParams(dimension_semantics=semantics, vmem_limit_bytes=VMEM_LIMIT_BYTES)


def _sigmoid(x):
    return 0.5 * jnp.tanh(0.5 * x) + 0.5


def _rms_norm_bf16(x, gain):
    ms = jnp.mean(x * x, axis=-1, keepdims=True)
    return (x * lax.rsqrt(ms + EPS) * gain).astype(BF16)


class _SideCast:
    def __init__(self, stacked, layer, n_steps, step_of):
        self.views = list(stacked)
        self.n = len(stacked)
        self.in_specs, self.out_specs, self.out_shapes = [], [], []
        self.ok = True
        for w in stacked:
            _, rows, cols = w.shape
            fits = [rb for rb in range(BF16_ROWS, rows + 1, BF16_ROWS)
                    if rows % rb == 0 and rb * n_steps >= rows]
            if not fits:
                self.ok = False
                continue
            rb = fits[0]
            last = rows // rb - 1
            self.in_specs.append(pl.BlockSpec(
                (None, rb, cols), lambda *g, last=last: (layer, jnp.minimum(step_of(*g), last), 0)))
            self.out_specs.append(pl.BlockSpec(
                (rb, cols), lambda *g, last=last: (jnp.minimum(step_of(*g), last), 0)))
            self.out_shapes.append(jax.ShapeDtypeStruct((rows, cols), BF16))

    @staticmethod
    def run(in_refs, out_refs):
        for src, dst in zip(in_refs, out_refs):
            dst[...] = src[...].astype(BF16)

    def finish(self, outs):
        return list(outs)


def _ffn_kernel(n_cast, h_ref, gain_ref, wg_ref, wu_ref, wd_ref, *rest):
    cast_in, o_ref = rest[:n_cast], rest[n_cast]
    cast_out, xn_ref = rest[n_cast + 1:2 * n_cast + 1], rest[-1]
    _SideCast.run(cast_in, cast_out)
    f = pl.program_id(1)

    @pl.when(f == 0)
    def _():
        x = h_ref[...]
        xn_ref[...] = _rms_norm_bf16(x, gain_ref[...])
        o_ref[...] = x

    xn = xn_ref[...]
    g = jnp.dot(xn, wg_ref[...], preferred_element_type=F32)
    u = jnp.dot(xn, wu_ref[...], preferred_element_type=F32)
    a = (0.5 * (g * _sigmoid(g)) * u).astype(BF16)
    o_ref[...] += jnp.dot(a, wd_ref[...], preferred_element_type=F32)


def _ffn(h, gain, layer, weights, cast_next=None):
    wg, wu, wd = weights
    rows, d = h.shape
    ff = wg.shape[-1]
    tm = _pick(rows, (1056, 768, 512, 384, 256, 128))
    tf = _pick(ff, (512, 256, 128))
    nf = ff // tf
    cast = None
    if cast_next is not None:
        cast = _SideCast(cast_next[0], cast_next[1], (rows // tm) * nf, lambda i, f: i * nf + f)
        if not cast.ok:
            cast = None
    n_cast = cast.n if cast else 0
    outs = pl.pallas_call(
        functools.partial(_ffn_kernel, n_cast),
        grid=(rows // tm, nf),
        in_specs=[
            pl.BlockSpec((tm, d), lambda i, f: (i, 0)),
            pl.BlockSpec((None, 1, d), lambda i, f: (layer, 0, 0)),
            pl.BlockSpec((d, tf), lambda i, f: (0, f)),
            pl.BlockSpec((d, tf), lambda i, f: (0, f)),
            pl.BlockSpec((tf, d), lambda i, f: (f, 0)),
        ] + (cast.in_specs if cast else []),
        out_specs=[pl.BlockSpec((tm, d), lambda i, f: (i, 0))] + (cast.out_specs if cast else []),
        out_shape=[jax.ShapeDtypeStruct((rows, d), F32)] + (cast.out_shapes if cast else []),
        scratch_shapes=[pltpu.VMEM((tm, d), BF16)],
        compiler_params=_params(("arbitrary", "arbitrary")),
        name="ffn",
    )(h, gain, wg, wu, wd, *(cast.views if cast else []))
    if cast:
        return outs[0], cast.finish(outs[1:])
    fallback = None if cast_next is None else [w[cast_next[1]].astype(BF16) for w in cast_next[0]]
    return outs[0], fallback


def _proj_kernel(kind, n_cast, xn_ref, w_ref, *rest):
    n_extra = 0 if kind == "plain" else 1
    cast_in = rest[n_extra:n_extra + n_cast]
    o_ref = rest[n_extra + n_cast]
    _SideCast.run(cast_in, rest[n_extra + n_cast + 1:])
    _proj_rows(kind, xn_ref, w_ref, rest, o_ref)


def _norm_rope_kernel(h_ref, gain_ref, w_ref, tab_ref, o_ref, xn_ref):
    @pl.when(pl.program_id(1) == 0)
    def _():
        xn_ref[...] = _rms_norm_bf16(h_ref[...], gain_ref[...])

    _proj_rows("rope", xn_ref, w_ref, (tab_ref,), o_ref)


def _proj_rows(kind, xn_ref, w_ref, rest, o_ref):
    tm, tn = o_ref.shape
    chunks = {"rope": 6 if tm % (6 * BF16_ROWS) == 0 else 4, "gate": 2, "plain": 1}[kind]
    cm = tm // chunks
    for r in range(chunks):
        rows = slice(r * cm, (r + 1) * cm)
        z = jnp.dot(xn_ref[rows, :], w_ref[...], preferred_element_type=F32)
        if kind == "rope":
            tab_ref = rest[0]
            reps = tn // LANES
            cos = jnp.concatenate([tab_ref[0, rows, :]] * reps, axis=1)
            sin_lo = jnp.concatenate([tab_ref[1, rows, :]] * reps, axis=1)
            sin_hi = jnp.concatenate([tab_ref[2, rows, :]] * reps, axis=1)
            half = ROT_DIM // 2
            z = z * cos + pltpu.roll(z, half, 1) * sin_lo + pltpu.roll(z, tn - half, 1) * sin_hi
        elif kind == "gate":
            z = _sigmoid(z + rest[0][...])
        o_ref[rows, :] = z.astype(o_ref.dtype)


def _proj(xn, w, layer, col0, ncols, out_dtype, kind, extra=None, cast_next=None):
    rows, d = xn.shape
    tm = _pick(rows, (2112, 1408, 768, 384, 128))
    tn = _pick(ncols, (512, 256, 128))
    assert col0 % tn == 0
    cb0 = col0 // tn
    in_specs = [
        pl.BlockSpec((tm, d), lambda i, n: (i, 0)),
        pl.BlockSpec((d, tn), lambda i, n: (0, cb0 + n)),
    ]
    args = [xn, w]
    if kind == "gate":
        in_specs.append(pl.BlockSpec((None, 1, tn), lambda i, n: (layer, 0, n)))
        args.append(extra)
    nn = ncols // tn
    cast = None
    if cast_next is not None:
        cast = _SideCast(cast_next[0], cast_next[1], (rows // tm) * nn, lambda i, n: i * nn + n)
        if not cast.ok:
            cast = None
    outs = pl.pallas_call(
        functools.partial(_proj_kernel, kind, cast.n if cast else 0),
        grid=(rows // tm, nn),
        in_specs=in_specs + (cast.in_specs if cast else []),
        out_specs=[pl.BlockSpec((tm, tn), lambda i, n: (i, n))] + (cast.out_specs if cast else []),
        out_shape=[jax.ShapeDtypeStruct((rows, ncols), out_dtype)] + (cast.out_shapes if cast else []),
        compiler_params=_params(("arbitrary", "arbitrary")),
        name="proj_" + kind,
    )(*args, *(cast.views if cast else []))
    if cast_next is None:
        return outs[0]
    if cast:
        return outs[0], cast.finish(outs[1:])
    return outs[0], [w[cast_next[1]].astype(BF16) for w in cast_next[0]]


def _norm_rope_proj(h, gain, layer, w, ncols, tables, seq_len):
    rows, d = h.shape
    tm = _pick(seq_len, (1056, 768, 384, 128))
    tn = _pick(ncols // 2, (512, 256, 128))
    tiles_per_seq = seq_len // tm
    q_blocks = ncols // 2 // tn
    return pl.pallas_call(
        _norm_rope_kernel,
        grid=(rows // tm, ncols // tn),
        in_specs=[
            pl.BlockSpec((tm, d), lambda i, n: (i, 0)),
            pl.BlockSpec((None, 1, d), lambda i, n: (layer, 0, 0)),
            pl.BlockSpec((d, tn), lambda i, n: (0, n)),
            pl.BlockSpec((None, 3, tm, LANES), lambda i, n: (n // q_blocks, 0, i % tiles_per_seq, 0)),
        ],
        out_specs=[pl.BlockSpec((tm, tn), lambda i, n: (i, n)),
                   pl.BlockSpec((tm, d), lambda i, n: (i, 0))],
        out_shape=[jax.ShapeDtypeStruct((rows, ncols), BF16), jax.ShapeDtypeStruct((rows, d), BF16)],
        compiler_params=_params(("arbitrary", "arbitrary")),
        name="norm_proj_rope",
    )(h, gain, w, tables)


def _rope_tables(seq_len):
    half = ROT_DIM // 2
    inv_freq = ROPE_THETA ** (-jnp.arange(0, ROT_DIM, 2, dtype=F32) / ROT_DIM)
    ang = jnp.arange(seq_len, dtype=F32)[:, None] * inv_freq[None, :]
    cos, sin = jnp.cos(ang), jnp.sin(ang)
    ones = jnp.ones((seq_len, LANES - ROT_DIM), F32)
    zeros = jnp.zeros((seq_len, LANES - ROT_DIM), F32)
    zh = jnp.zeros((seq_len, half), F32)
    c = jnp.concatenate([cos, cos, ones], axis=1)
    s_lo = jnp.concatenate([zh, sin, zeros], axis=1)
    s_hi = jnp.concatenate([-sin, zh, zeros], axis=1)
    k_tab = jnp.stack([c, s_lo, s_hi])
    return jnp.stack([k_tab * ((HEAD_DIM ** -0.5) * math.log2(math.e)), k_tab])


def _attn_stages(k_ref, v_ref, qs, m_ref, l_ref, acc_ref, p_refs, a_refs):
    tq = qs[0].shape[0]
    rows_per_dot = min(tq, ATTN_ROWS_PER_DOT)

    def score_stage(start, width, masked, slot):
        kb = k_ref[pl.ds(start, width), :]
        for c in range(2):
            kc = kb[:, c * HEAD_DIM:(c + 1) * HEAD_DIM]
            for r0 in range(0, tq, rows_per_dot):
                rows = slice(r0, r0 + rows_per_dot)
                kw = min(width, r0 + rows_per_dot) if masked else width
                s = lax.dot_general(qs[c][rows], kc[0:kw], (((1,), (1,)), ((), ())),
                                    preferred_element_type=F32)
                if masked:
                    row = lax.broadcasted_iota(jnp.int32, s.shape, 0) + r0
                    col = lax.broadcasted_iota(jnp.int32, s.shape, 1)
                    s = jnp.where(col <= row, s, NEG_BIG)
                m_old = m_ref[c, rows, :]
                m_new = jnp.maximum(m_old, jnp.max(s, axis=-1, keepdims=True))
                alpha = jnp.exp2(m_old - m_new)
                p = jnp.exp2(s - jnp.tile(m_new, (1, kw // LANES)))
                l_ref[c, rows, :] = alpha * l_ref[c, rows, :] + jnp.sum(p, axis=-1, keepdims=True)
                m_ref[c, rows, :] = m_new
                a_refs[slot][c, rows, :] = alpha
                p_refs[slot][c, rows, 0:kw] = p.astype(BF16)
                if kw < width:
                    p_refs[slot][c, rows, kw:width] = jnp.zeros((rows_per_dot, width - kw), BF16)

    def value_stage(start, width, slot):
        vb = v_ref[pl.ds(start, width), :]
        for c in range(2):
            for r0 in range(0, tq, rows_per_dot):
                rows = slice(r0, r0 + rows_per_dot)
                pv = jnp.dot(p_refs[slot][c, rows, 0:width], vb, preferred_element_type=F32)
                acc_ref[c, rows, :] = (jnp.tile(a_refs[slot][c, rows, :], (1, V_DIM // LANES))
                                       * acc_ref[c, rows, :] + pv)

    return score_stage, value_stage


def _attn_init(m_ref, l_ref, acc_ref):
    m_ref[...] = jnp.full(m_ref.shape, NEG_BIG, F32)
    l_ref[...] = jnp.zeros(l_ref.shape, F32)
    acc_ref[...] = jnp.zeros(acc_ref.shape, F32)


def _attn_finish(lam_init, lq1_ref, lk1_ref, lq2_ref, lk2_ref, subln_ref, o_ref, l_ref, acc_ref):
    lam = (jnp.exp(jnp.sum(lq1_ref[...] * lk1_ref[...], keepdims=True))
           - jnp.exp(jnp.sum(lq2_ref[...] * lk2_ref[...], keepdims=True)) + lam_init)
    reps = V_DIM // LANES
    o = (acc_ref[0] * jnp.tile(1.0 / l_ref[0], (1, reps))
         - lam * (acc_ref[1] * jnp.tile(1.0 / l_ref[1], (1, reps))))
    o = o * lax.rsqrt(jnp.mean(o * o, axis=-1, keepdims=True) + EPS)
    o = o * subln_ref[...] * (1.0 - lam_init)
    o_ref[...] = o.astype(o_ref.dtype)


def _attn_main_kernel(lam_init, n_cast, q_ref, k_ref, v_ref, lq1_ref, lk1_ref, lq2_ref, lk2_ref,
                      subln_ref, *rest):
    cast_in, o_ref, cast_out = rest[:n_cast], rest[n_cast], rest[n_cast + 1:2 * n_cast + 1]
    m_ref, l_ref, acc_ref, p0_ref, p1_ref, a0_ref, a1_ref = rest[2 * n_cast + 1:]
    _SideCast.run(cast_in, cast_out)
    i = pl.program_id(2)
    tq = q_ref.shape[0]
    q = q_ref[...]
    score_stage, value_stage = _attn_stages(k_ref, v_ref, (q[:, :HEAD_DIM], q[:, HEAD_DIM:]),
                                            m_ref, l_ref, acc_ref, (p0_ref, p1_ref), (a0_ref, a1_ref))
    _attn_init(m_ref, l_ref, acc_ref)
    score_stage(pl.multiple_of(i * tq, tq), tq, True, 0)

    def chunk_start(n):
        return pl.multiple_of(jnp.where(n == 0, i, n - 1) * tq, tq)

    def body(n, carry):
        for slot in range(2):
            @pl.when(n % 2 == slot)
            def _():
                score_stage(chunk_start(n), tq, False, slot)
                value_stage(chunk_start(n - 1), tq, 1 - slot)
        return carry

    lax.fori_loop(1, i + 1, body, 0)
    for slot in range(2):
        @pl.when(i % 2 == slot)
        def _():
            value_stage(chunk_start(i), tq, slot)
    _attn_finish(lam_init, lq1_ref, lk1_ref, lq2_ref, lk2_ref, subln_ref, o_ref, l_ref, acc_ref)


def _attn_tail_kernel(lam_init, tk, q_ref, k_ref, v_ref, lq1_ref, lk1_ref, lq2_ref, lk2_ref, subln_ref,
                      o_in_ref, o_ref, m_ref, l_ref, acc_ref, p0_ref, p1_ref, a0_ref, a1_ref):
    del o_in_ref
    tq = q_ref.shape[0]
    n_full = (k_ref.shape[0] - tq) // tk
    q = q_ref[...]
    score_stage, value_stage = _attn_stages(k_ref, v_ref, (q[:, :HEAD_DIM], q[:, HEAD_DIM:]),
                                            m_ref, l_ref, acc_ref, (p0_ref, p1_ref), (a0_ref, a1_ref))
    _attn_init(m_ref, l_ref, acc_ref)
    chunks = [(n_full * tk, tq)] + [(j * tk, tk) for j in range(n_full)]
    score_stage(chunks[0][0], chunks[0][1], True, 0)
    for n in range(1, len(chunks)):
        score_stage(chunks[n][0], chunks[n][1], False, n % 2)
        value_stage(chunks[n - 1][0], chunks[n - 1][1], (n - 1) % 2)
    value_stage(chunks[-1][0], chunks[-1][1], (len(chunks) - 1) % 2)
    _attn_finish(lam_init, lq1_ref, lk1_ref, lq2_ref, lk2_ref, subln_ref, o_ref, l_ref, acc_ref)


def _attn_scratch(tq, tk):
    return [
        pltpu.VMEM((2, tq, LANES), F32),
        pltpu.VMEM((2, tq, LANES), F32),
        pltpu.VMEM((2, tq, V_DIM), F32),
        pltpu.VMEM((2, tq, tk), BF16),
        pltpu.VMEM((2, tq, tk), BF16),
        pltpu.VMEM((2, tq, LANES), F32),
        pltpu.VMEM((2, tq, LANES), F32),
    ]


def _attention(qk, v, lq1, lk1, lq2, lk2, subln, layer, lam_init, batch, seq_len, cast_next=None):
    n_heads = v.shape[1] // V_DIM
    tk = ATTN_CHUNK
    nq = seq_len // tk
    rem = seq_len - nq * tk
    qk3 = qk.reshape(batch, seq_len, qk.shape[1])
    v3 = v.reshape(batch, seq_len, v.shape[1])
    out_shape = jax.ShapeDtypeStruct((batch, seq_len, n_heads * V_DIM), BF16)

    def kv_param_specs(grid_rank):
        def at(*idx):
            return (lambda b, h, i: idx_of(b, h, idx)) if grid_rank == 3 else (lambda b, h: idx_of(b, h, idx))

        def idx_of(b, h, idx):
            return tuple(f(b, h) if callable(f) else f for f in idx)

        lam_spec = pl.BlockSpec((None, 1, HEAD_DIM), at(layer, 0, 0))
        return [pl.BlockSpec((None, seq_len, V_DIM), at(lambda b, h: b, 0, lambda b, h: n_heads + h)),
                pl.BlockSpec((None, seq_len, V_DIM), at(lambda b, h: b, 0, lambda b, h: h)),
                lam_spec, lam_spec, lam_spec, lam_spec,
                pl.BlockSpec((None, 1, V_DIM), at(layer, 0, 0))]

    cast = None
    if cast_next is not None:
        cast = _SideCast(cast_next[0], cast_next[1], batch * n_heads * nq,
                         lambda b, h, i: (b * n_heads + h) * nq + i)
        if not cast.ok:
            cast = None
    outs = pl.pallas_call(
        functools.partial(_attn_main_kernel, lam_init, cast.n if cast else 0),
        grid=(batch, n_heads, nq),
        in_specs=([pl.BlockSpec((None, tk, V_DIM), lambda b, h, i: (b, i, h))] + kv_param_specs(3)
                  + (cast.in_specs if cast else [])),
        out_specs=[pl.BlockSpec((None, tk, V_DIM), lambda b, h, i: (b, i, h))] + (cast.out_specs if cast else []),
        out_shape=[out_shape] + (cast.out_shapes if cast else []),
        scratch_shapes=_attn_scratch(tk, tk),
        compiler_params=_params(("arbitrary", "arbitrary", "arbitrary")),
        name="attn_main",
    )(qk3, qk3, v3, lq1, lk1, lq2, lk2, subln, *(cast.views if cast else []))
    out = outs[0]
    if cast:
        cast_out = cast.finish(outs[1:])
    else:
        cast_out = None if cast_next is None else [w[cast_next[1]].astype(BF16) for w in cast_next[0]]
    if rem:
        assert seq_len % rem == 0
        last = seq_len // rem - 1
        out = pl.pallas_call(
            functools.partial(_attn_tail_kernel, lam_init, tk),
            grid=(batch, n_heads),
            in_specs=([pl.BlockSpec((None, rem, V_DIM), lambda b, h: (b, last, h))] + kv_param_specs(2)
                      + [pl.BlockSpec(memory_space=pl.ANY)]),
            out_specs=pl.BlockSpec((None, rem, V_DIM), lambda b, h: (b, last, h)),
            out_shape=out_shape,
            scratch_shapes=_attn_scratch(rem, tk),
            input_output_aliases={8: 0},
            compiler_params=_params(("arbitrary", "arbitrary")),
            name="attn_tail",
        )(qk3, qk3, v3, lq1, lk1, lq2, lk2, subln, out)
    return out.reshape(batch * seq_len, n_heads * V_DIM), cast_out


def _lru_kernel(n_cast, xr_ref, yr_ref, p_ref, w_ref, *rest):
    cast_in, o_ref, cast_out = rest[:n_cast], rest[n_cast], rest[n_cast + 1:2 * n_cast + 1]
    halo_ref, carry_ref, a_ref, b_ref, h_ref = rest[2 * n_cast + 1:]
    _SideCast.run(cast_in, cast_out)
    t = pl.program_id(2)
    tt, cw = xr_ref.shape
    halo = SUBLANES

    @pl.when(t == 0)
    def _():
        halo_ref[...] = jnp.zeros(halo_ref.shape, F32)
        carry_ref[...] = jnp.zeros(carry_ref.shape, F32)

    prm = p_ref[...]
    conv_b, ga_b, gx_b = prm[0:1], prm[1:2], prm[2:3]
    half_scale = (0.5 * LRU_C) * jax.nn.log_sigmoid(prm[3:4])
    row8 = lax.broadcasted_iota(jnp.int32, (halo, cw), 0)
    nl = cw // LANES
    rc = next(tt // n for n in range(2, tt) if tt % (n * SUBLANES) == 0 and tt // n <= LRU_CHUNK_ROWS)
    ng = tt // SUBLANES
    assert ng % SUBLANES == 0 and rc % SUBLANES == 0

    def scan_rows(t0):
        s, g0 = divmod(t0, ng)
        return pl.ds(g0 * SUBLANES + s, SUBLANES, stride=SUBLANES)

    for c0 in range(0, tt, rc):
        x = xr_ref[c0:c0 + rc, :]
        prev = halo_ref[...] if c0 == 0 else xr_ref[c0 - halo:c0, :]
        xc = conv_b + x * prm[4:5]
        for j in range(1, CONV_W):
            xs = pltpu.roll(x, j, 0)
            head = jnp.where(row8 < j, pltpu.roll(prev, j, 0), xs[0:halo, :])
            xc = xc + jnp.concatenate([head, xs[halo:, :]], axis=0) * prm[4 + j:5 + j]

        gz = jnp.dot(xc.astype(BF16), w_ref[...], preferred_element_type=F32)
        tr = jnp.tanh(gz[:, :cw] + ga_b)
        ti = jnp.tanh(gz[:, cw:] + gx_b)
        log_a = tr * half_scale + half_scale
        a = jnp.exp(log_a)
        y = -jnp.tanh(log_a) * (a * a + 1.0)
        mult = jnp.where(y > 0.0, y * lax.rsqrt(y), 0.0)
        xh = 0.5 * xc
        bx = mult * (ti * xh + xh)
        for r0 in range(0, rc, SUBLANES):
            dst = scan_rows(c0 + r0)
            for k in range(nl):
                a_ref[k, dst, :] = a[r0:r0 + SUBLANES, k * LANES:(k + 1) * LANES]
                b_ref[k, dst, :] = bx[r0:r0 + SUBLANES, k * LANES:(k + 1) * LANES]
    halo_ref[...] = xr_ref[tt - halo:tt, :]

    def group(ref, g):
        rows = pl.ds(pl.multiple_of(g * SUBLANES, SUBLANES), SUBLANES)
        return jnp.concatenate([ref[k, rows, :] for k in range(nl)], axis=1)

    def put_group(ref, g, val):
        rows = pl.ds(pl.multiple_of(g * SUBLANES, SUBLANES), SUBLANES)
        for k in range(nl):
            ref[k, rows, :] = val[:, k * LANES:(k + 1) * LANES]

    def block_scan(g, state):
        h, cum = state
        ag = group(a_ref, g)
        h = ag * h + group(b_ref, g)
        cum = ag * cum
        put_group(h_ref, g, h)
        put_group(a_ref, g, cum)
        return h, cum

    h, cum = lax.fori_loop(0, ng, block_scan,
                           (jnp.zeros((SUBLANES, cw), F32), jnp.ones((SUBLANES, cw), F32)),
                           unroll=LRU_SCAN_UNROLL)
    e = carry_ref[...]
    entry = []
    for s in range(SUBLANES):
        entry.append(e)
        e = cum[s:s + 1, :] * e + h[s:s + 1, :]
    carry_ref[...] = e
    entry = jnp.concatenate(entry, axis=0)

    def block_fix(g, carry):
        put_group(h_ref, g, group(h_ref, g) + group(a_ref, g) * entry)
        return carry

    lax.fori_loop(0, ng, block_fix, 0, unroll=LRU_SCAN_UNROLL)

    for c0 in range(0, tt, rc):
        hs = jnp.concatenate(
            [jnp.concatenate([h_ref[k, scan_rows(c0 + r0), :] for k in range(nl)], axis=1)
             for r0 in range(0, rc, SUBLANES)], axis=0)
        o_ref[c0:c0 + rc, :] = (hs * jax.nn.gelu(yr_ref[c0:c0 + rc, :])).astype(o_ref.dtype)


def _lru(z_lru, prm, w_gate, layer, batch, seq_len, cast_next=None):
    rows = z_lru.shape[0]
    width = z_lru.shape[1] // 2
    cw = LRU_SUPER
    ns = width // cw
    tt = _pick(seq_len, LRU_TILE_ROWS)
    nt = seq_len // tt
    cast = None
    if cast_next is not None:
        cast = _SideCast(cast_next[0], cast_next[1], batch * ns * nt,
                         lambda b, s, t: (b * ns + s) * nt + t)
        if not cast.ok:
            cast = None
    outs = pl.pallas_call(
        functools.partial(_lru_kernel, cast.n if cast else 0),
        grid=(batch, ns, nt),
        in_specs=[
            pl.BlockSpec((tt, cw), lambda b, s, t: (b * nt + t, s)),
            pl.BlockSpec((tt, cw), lambda b, s, t: (b * nt + t, ns + s)),
            pl.BlockSpec((None, 2 * CONV_W, cw), lambda b, s, t: (layer, 0, s)),
            pl.BlockSpec((None, None, cw, 2 * cw), lambda b, s, t: (layer, s, 0, 0)),
        ] + (cast.in_specs if cast else []),
        out_specs=[pl.BlockSpec((tt, cw), lambda b, s, t: (b * nt + t, s))] + (cast.out_specs if cast else []),
        out_shape=[jax.ShapeDtypeStruct((rows, width), BF16)] + (cast.out_shapes if cast else []),
        scratch_shapes=[
            pltpu.VMEM((SUBLANES, cw), F32),
            pltpu.VMEM((1, cw), F32),
            pltpu.VMEM((cw // LANES, tt, LANES), F32),
            pltpu.VMEM((cw // LANES, tt, LANES), F32),
            pltpu.VMEM((cw // LANES, tt, LANES), F32),
        ],
        compiler_params=_params(("arbitrary", "arbitrary", "arbitrary")),
        name="rglru",
    )(z_lru, z_lru, prm, w_gate, *(cast.views if cast else []))
    if cast:
        return outs[0], cast.finish(outs[1:])
    return outs[0], (None if cast_next is None else [w[cast_next[1]].astype(BF16) for w in cast_next[0]])


def _lru_gate_weights(ga_w, gx_w):
    depth, nb, bw, _ = ga_w.shape
    per = LRU_SUPER // bw
    ns = nb // per

    def dense(w):
        w = (0.5 * w).astype(BF16).reshape(depth, ns, per, bw, bw)
        rows = [jnp.pad(w[:, :, p], ((0, 0), (0, 0), (0, 0), (p * bw, (per - 1 - p) * bw)))
                for p in range(per)]
        return jnp.concatenate(rows, axis=2)

    return jnp.concatenate([dense(ga_w), dense(gx_w)], axis=-1)


def _mix_kernel(oa_ref, ol_ref, wa_ref, wl_ref, ga_ref, gl_ref, o_ref):
    ya = jnp.dot(oa_ref[...], wa_ref[...], preferred_element_type=F32)
    yl = jnp.dot(ol_ref[...], wl_ref[...], preferred_element_type=F32)
    o_ref[...] = (ga_ref[...] * ya + gl_ref[...] * yl).astype(o_ref.dtype)


def _mix(oa, ol, w_ba, w_bl, gates):
    rows, da = oa.shape
    dl = ol.shape[1]
    d = w_ba.shape[-1]
    tm = _pick(rows, (1408, 1056, 768, 384, 128))
    tn = _pick(d, (512, 256, 128))
    nb = d // tn
    return pl.pallas_call(
        _mix_kernel,
        grid=(rows // tm, nb),
        in_specs=[
            pl.BlockSpec((tm, da), lambda i, n: (i, 0)),
            pl.BlockSpec((tm, dl), lambda i, n: (i, 0)),
            pl.BlockSpec((da, tn), lambda i, n: (0, n)),
            pl.BlockSpec((dl, tn), lambda i, n: (0, n)),
            pl.BlockSpec((tm, tn), lambda i, n: (i, n)),
            pl.BlockSpec((tm, tn), lambda i, n: (i, nb + n)),
        ],
        out_specs=pl.BlockSpec((tm, tn), lambda i, n: (i, n)),
        out_shape=jax.ShapeDtypeStruct((rows, d), BF16),
        compiler_params=_params(("arbitrary", "arbitrary")),
        name="branch_mix",
    )(oa, ol, w_ba, w_bl, gates, gates)


def _out_kernel(y_ref, w_ref, h_ref, o_ref):
    o_ref[...] = h_ref[...] + jnp.dot(y_ref[...], w_ref[...], preferred_element_type=F32)


def _out_proj(y, w_o, h):
    rows, d = h.shape
    tm = _pick(rows, (2112, 1056, 768, 384, 128))
    tn = _pick(d, (512, 256, 128))
    return pl.pallas_call(
        _out_kernel,
        grid=(rows // tm, d // tn),
        in_specs=[
            pl.BlockSpec((tm, y.shape[1]), lambda i, n: (i, 0)),
            pl.BlockSpec((y.shape[1], tn), lambda i, n: (0, n)),
            pl.BlockSpec((tm, tn), lambda i, n: (i, n)),
        ],
        out_specs=pl.BlockSpec((tm, tn), lambda i, n: (i, n)),
        out_shape=jax.ShapeDtypeStruct((rows, d), F32),
        compiler_params=_params(("arbitrary", "arbitrary")),
        name="out_proj",
    )(y, w_o, h)


def _final_norm_kernel(h_ref, next_ref, g_ref, o_ref):
    x = jnp.concatenate([h_ref[N_META:, :], next_ref[...]], axis=0)
    ms = jnp.mean(x * x, axis=-1, keepdims=True)
    o_ref[...] = x * lax.rsqrt(ms + EPS) * g_ref[...]


def _final_norm(h, gain, batch, seq_len, seq):
    d = h.shape[1]
    tb = _pick(seq, (512, 256, 128))
    assert tb % N_META == 0 and seq + N_META <= seq_len
    h3 = h.reshape(batch, seq_len, d)
    return pl.pallas_call(
        _final_norm_kernel,
        grid=(batch, seq // tb),
        in_specs=[pl.BlockSpec((None, tb, d), lambda b, j: (b, j, 0)),
                  pl.BlockSpec((None, N_META, d), lambda b, j: (b, (j + 1) * (tb // N_META), 0)),
                  pl.BlockSpec((1, d), lambda b, j: (0, 0))],
        out_specs=pl.BlockSpec((None, tb, d), lambda b, j: (b, j, 0)),
        out_shape=jax.ShapeDtypeStruct((batch, seq, d), F32),
        compiler_params=_params(("arbitrary", "arbitrary")),
        name="final_norm",
    )(h3, h3, gain)


def kernel(x, meta_tokens, norm_ffn1, ffn1_w_gate, ffn1_w_up, ffn1_w_down, norm_mix, w_in, b_gate, lambda_q1, lambda_k1, lambda_q2, lambda_k2, attn_subln, conv_w, conv_b, gate_x_w, gate_x_b, gate_a_w, gate_a_b, lru_a_param, w_branch_attn, w_branch_lru, w_out, norm_ffn2, ffn2_w_gate, ffn2_w_up, ffn2_w_down, final_norm):
    batch, seq, d = x.shape
    depth = w_in.shape[0]
    attn_v = w_branch_attn.shape[1]
    attn_qk = attn_v
    lru_w = w_branch_lru.shape[1]
    seq_real = N_META + seq
    tp = -(-seq_real // SEQ_BLOCK) * SEQ_BLOCK
    rows = batch * tp

    meta = jnp.broadcast_to(meta_tokens.astype(x.dtype)[None], (batch, N_META, d))
    h = jnp.concatenate([meta, x, jnp.zeros((batch, tp - seq_real, d), x.dtype)], axis=1)
    h = h.reshape(rows, d)

    def vec(p):
        return p[:, None, :]

    bf = lambda w: w.astype(BF16)
    ffn1_f32 = (ffn1_w_gate, ffn1_w_up, ffn1_w_down)
    ffn2_f32 = (ffn2_w_gate, ffn2_w_up, ffn2_w_down)
    f1 = [bf(w[0]) for w in ffn1_f32]
    f2 = None
    w_in_l = bf(w_in[0])
    branch_f32 = (w_branch_attn, w_branch_lru, w_out)
    w_ba = w_bl = w_o = None
    w_lru_gate = _lru_gate_weights(gate_a_w, gate_x_w)
    lru_prm = jnp.concatenate([conv_b[:, None], 0.5 * gate_a_b[:, None], 0.5 * gate_x_b[:, None],
                               lru_a_param[:, None], conv_w], axis=1)
    rope = _rope_tables(tp)
    n1, nm, n2 = vec(norm_ffn1), vec(norm_mix), vec(norm_ffn2)
    bg = vec(b_gate)
    lq1, lk1, lq2, lk2 = vec(lambda_q1), vec(lambda_k1), vec(lambda_q2), vec(lambda_k2)
    subln = vec(attn_subln)

    c_v = 2 * attn_qk
    c_lru = c_v + attn_v
    c_gate = c_lru + 2 * lru_w
    for l in range(depth):
        lam_init = 0.8 - 0.6 * math.exp(-0.3 * l)
        more = l + 1 < depth
        h, f1 = _ffn(h, n1, l, f1, (ffn1_f32, l + 1) if more else None)
        qk, xn = _norm_rope_proj(h, nm, l, w_in_l, 2 * attn_qk, rope, tp)
        if l == 0:
            v, (w_ba, w_bl, w_o) = _proj(xn, w_in_l, l, c_v, attn_v, BF16, "plain",
                                         cast_next=(branch_f32, 0))
            gates, f2 = _proj(xn, w_in_l, l, c_gate, 2 * d, F32, "gate", bg, cast_next=(ffn2_f32, 0))
        else:
            v = _proj(xn, w_in_l, l, c_v, attn_v, BF16, "plain")
            gates = _proj(xn, w_in_l, l, c_gate, 2 * d, F32, "gate", bg)
        z_lru = _proj(xn, w_in_l, l, c_lru, 2 * lru_w, F32, "plain")
        o_attn, w_in_next = _attention(qk, v, lq1, lk1, lq2, lk2, subln, l, lam_init, batch, tp,
                                       ((w_in,), l + 1) if more else None)
        if more:
            w_in_l = w_in_next[0]
        o_lru, branch_next = _lru(z_lru, lru_prm, w_lru_gate, l, batch, tp,
                                  (branch_f32, l + 1) if more else None)
        y = _mix(o_attn, o_lru, w_ba, w_bl, gates)
        h = _out_proj(y, w_o, h)
        if more:
            w_ba, w_bl, w_o = branch_next
        h, f2 = _ffn(h, n2, l, f2, (ffn2_f32, l + 1) if more else None)
    return _final_norm(h, final_norm[None, :], batch, tp, seq)
```
